```python
import math
import jax, jax.numpy as jnp
from jax import lax
import numpy as np

D_MODEL = 2048
BATCH = 8
SEQ = 2048
DEPTH = 1

ATTN_HEAD_DIM = 64
ATTN_HEADS = (D_MODEL // 2) // ATTN_HEAD_DIM
ATTN_KV_HEADS = 4
WINDOW = 128
ATTN_BLOCK = 128
DN_HEAD_DIM = 128
DN_HEADS = (D_MODEL // 2) // DN_HEAD_DIM
DN_CHUNK = 64
CONV_WIDTH = 4

ATTN_Q_W = ATTN_HEADS * ATTN_HEAD_DIM
ATTN_KV_W = ATTN_KV_HEADS * ATTN_HEAD_DIM
DN_W = DN_HEADS * DN_HEAD_DIM
MIX_WIDTH = ATTN_Q_W + DN_W
IN_WIDTHS = (ATTN_Q_W, ATTN_KV_W, ATTN_KV_W, DN_W, DN_W, DN_W, DN_W, DN_HEADS, DN_HEADS)
IN_WIDTH = sum(IN_WIDTHS)

N_EXPERTS = 32
TOP_K = 4
D_EXPERT = D_MODEL
SWIGLU_ALPHA = 1.702
SWIGLU_LIMIT = 7.0
MOE_BLOCK = 256

NORM_EPS = 1e-5
QK_NORM_EPS = 1e-6
DN_NORM_EPS = 1e-6
L2_EPS = 1e-6

kernel_name = "hybrid_swa_sink_gdn_moe_layer"


def rms_norm(x, w, eps):
    xf = x.astype(jnp.float32)
    y = xf * lax.rsqrt(jnp.mean(xf * xf, axis=-1, keepdims=True) + eps)
    return (y * w.astype(jnp.float32)).astype(x.dtype)


def l2_normalize(x):
    return x * lax.rsqrt(jnp.sum(x * x, axis=-1, keepdims=True) + L2_EPS)


def alibi_slopes(n):
    return 2.0 ** (-8.0 * jnp.arange(1, n + 1, dtype=jnp.float32) / n)


def sliding_window_attention(q, k, v, sinks):
    B, S = q.shape[0], q.shape[1]
    L = ATTN_BLOCK
    nb = S // L
    G = ATTN_HEADS // ATTN_KV_HEADS
    qb = q.reshape(B, nb, L, ATTN_KV_HEADS, G, ATTN_HEAD_DIM)
    pad = ((0, 0), (L, 0), (0, 0), (0, 0))
    kp = jnp.pad(k, pad).reshape(B, nb + 1, L, ATTN_KV_HEADS, ATTN_HEAD_DIM)
    vp = jnp.pad(v, pad).reshape(B, nb + 1, L, ATTN_KV_HEADS, ATTN_HEAD_DIM)
    kw = jnp.concatenate([kp[:, :-1], kp[:, 1:]], axis=2)
    vw = jnp.concatenate([vp[:, :-1], vp[:, 1:]], axis=2)
    scores = jnp.einsum('bnqhgd,bnkhd->bnhgqk', qb, kw,
                        preferred_element_type=jnp.float32) * (ATTN_HEAD_DIM ** -0.5)
    q_pos = jnp.arange(S, dtype=jnp.int32).reshape(nb, L)
    k_pos = jnp.arange(-L, S, dtype=jnp.int32).reshape(nb + 1, L)
    k_win = jnp.concatenate([k_pos[:-1], k_pos[1:]], axis=1)
    dist = q_pos[:, :, None] - k_win[:, None, :]
    valid = (dist >= 0) & (dist < WINDOW) & (k_win[:, None, :] >= 0)
    slopes = alibi_slopes(ATTN_HEADS).reshape(ATTN_KV_HEADS, G)
    bias = -slopes[None, None, :, :, None, None] * dist.astype(jnp.float32)[None, :, None, None]
    scores = jnp.where(valid[None, :, None, None], scores + bias, -jnp.inf)
    sink = sinks.astype(jnp.float32).reshape(ATTN_KV_HEADS, G)[None, None, :, :, None, None]
    m = jnp.maximum(jnp.max(scores, axis=-1, keepdims=True), sink)
    p = jnp.exp(scores - m)
    denom = jnp.sum(p, axis=-1, keepdims=True) + jnp.exp(sink - m)
    probs = (p / denom).astype(v.dtype)
    out = jnp.einsum('bnhgqk,bnkhd->bnqhgd', probs, vw)
    return out.reshape(B, S, ATTN_Q_W)


def causal_depthwise_conv(x, w):
    K, C = w.shape
    return lax.conv_general_dilated(x, w[:, None, :], window_strides=(1,),
                                    padding=[(K - 1, 0)],
                                    dimension_numbers=('NWC', 'WIO', 'NWC'),
                                    feature_group_count=C)


def chunked_gated_delta_rule(q, k, v, g, beta):
    B, H, S, dk = q.shape
    dv = v.shape[-1]
    C = DN_CHUNK
    n = S // C
    q = q * (dk ** -0.5)
    qc = q.reshape(B, H, n, C, dk)
    kc = k.reshape(B, H, n, C, dk)
    vc = v.reshape(B, H, n, C, dv)
    bc = beta.reshape(B, H, n, C)
    gc = jnp.cumsum(g.reshape(B, H, n, C), axis=-1)
    tril = jnp.tril(jnp.ones((C, C), dtype=bool))
    strict = jnp.tril(jnp.ones((C, C), dtype=bool), -1)
    decay = jnp.exp(jnp.where(tril, gc[..., :, None] - gc[..., None, :], -jnp.inf))
    k_beta = kc * bc[..., None]
    v_beta = vc * bc[..., None]
    lower = jnp.where(strict, jnp.einsum('bhncd,bhnsd->bhncs', k_beta, kc) * decay, 0.0)
    a_mat = lower + jnp.eye(C, dtype=jnp.float32)
    rhs = jnp.concatenate([v_beta, k_beta * jnp.exp(gc)[..., None]], axis=-1)
    sol = lax.linalg.triangular_solve(a_mat, rhs, left_side=True, lower=True, unit_diagonal=True)
    u = sol[..., :dv]
    w = sol[..., dv:]
    attn_intra = jnp.where(tril, jnp.einsum('bhncd,bhnsd->bhncs', qc, kc) * decay, 0.0)

    def step(state, xs):
        q_i, k_i, u_i, w_i, g_i, a_i = xs
        v_new = u_i - jnp.matmul(w_i, state)
        o_i = jnp.matmul(q_i * jnp.exp(g_i)[..., None], state) + jnp.matmul(a_i, v_new)
        g_last = g_i[..., -1]
        k_dec = k_i * jnp.exp(g_last[..., None] - g_i)[..., None]
        state = state * jnp.exp(g_last)[..., None, None] + jnp.einsum('bhcd,bhce->bhde', k_dec, v_new)
        return state, o_i

    xs = tuple(jnp.moveaxis(t, 2, 0) for t in (qc, kc, u, w, gc, attn_intra))
    state0 = jnp.zeros((B, H, dk, dv), jnp.float32)
    _, o = lax.scan(step, state0, xs)
    return jnp.moveaxis(o, 0, 2).reshape(B, H, S, dv)


def gated_deltanet(dq, dk, dv, dz, db, da, conv_w, a_log, dt_bias, norm_w):
    B, S = dq.shape[0], dq.shape[1]
    qkv = jax.nn.silu(causal_depthwise_conv(jnp.concatenate([dq, dk, dv], axis=-1), conv_w))
    q, k, v = jnp.split(qkv.astype(jnp.float32), 3, axis=-1)
    heads = lambda t: t.reshape(B, S, DN_HEADS, DN_HEAD_DIM).transpose(0, 2, 1, 3)
    q = l2_normalize(heads(q))
    k = l2_normalize(heads(k))
    v = heads(v)
    beta = jax.nn.sigmoid(db.astype(jnp.float32)).transpose(0, 2, 1)
    g = (-jnp.exp(a_log.astype(jnp.float32))
         * jax.nn.softplus(da.astype(jnp.float32) + dt_bias.astype(jnp.float32))).transpose(0, 2, 1)
    o = chunked_gated_delta_rule(q, k, v, g, beta).transpose(0, 2, 1, 3)
    z = dz.astype(jnp.float32).reshape(B, S, DN_HEADS, DN_HEAD_DIM)
    o = rms_norm(o, norm_w, DN_NORM_EPS) * jax.nn.silu(z)
    return o.reshape(B, S, DN_W).astype(dq.dtype)


def moe_ffn(x, w_router, b_router, w_gate_up, b_gate_up, w_down, b_down):
    B, S, D = x.shape
    T = B * S
    xf = x.reshape(T, D)
    logits = jnp.matmul(xf, w_router).astype(jnp.float32) + b_router.astype(jnp.float32)
    top_vals, top_idx = lax.top_k(logits, TOP_K)
    gates = jax.nn.softmax(top_vals, axis=-1)
    n_assign = T * TOP_K
    e_flat = top_idx.reshape(n_assign)
    tok_flat = jnp.repeat(jnp.arange(T, dtype=jnp.int32), TOP_K)
    g_flat = gates.reshape(n_assign)
    order = jnp.argsort(e_flat)
    e_sorted = e_flat[order]
    counts = jnp.bincount(e_flat, length=N_EXPERTS)
    padded = (counts + MOE_BLOCK - 1) // MOE_BLOCK * MOE_BLOCK
    starts = jnp.cumsum(counts) - counts
    ends_p = jnp.cumsum(padded)
    pstarts = ends_p - padded
    dest = pstarts[e_sorted] + (jnp.arange(n_assign, dtype=jnp.int32) - starts[e_sorted])
    n_blocks = -(-n_assign // MOE_BLOCK) + N_EXPERTS
    P = n_blocks * MOE_BLOCK
    row_tok = jnp.full((P,), T, jnp.int32).at[dest].set(tok_flat[order])
    row_gate = jnp.zeros((P,), jnp.float32).at[dest].set(g_flat[order])
    block_expert = jnp.minimum(
        jnp.searchsorted(ends_p, jnp.arange(n_blocks, dtype=jnp.int32) * MOE_BLOCK, side='right'),
        N_EXPERTS - 1)
    x_pad = jnp.concatenate([xf, jnp.zeros((1, D), xf.dtype)], axis=0)

    def expert_block(args):
        e, toks, gts = args
        h = jnp.matmul(x_pad[toks], w_gate_up[e]) + b_gate_up[e]
        h_glu, h_lin = jnp.split(h, 2, axis=-1)
        h_glu = jnp.minimum(h_glu, SWIGLU_LIMIT)
        h_lin = jnp.clip(h_lin, -SWIGLU_LIMIT, SWIGLU_LIMIT)
        act = h_glu * jax.nn.sigmoid(SWIGLU_ALPHA * h_glu) * (h_lin + 1.0)
        y = jnp.matmul(act, w_down[e]) + b_down[e]
        return y.astype(jnp.float32) * gts[:, None]

    y = lax.map(expert_block, (block_expert, row_tok.reshape(n_blocks, MOE_BLOCK),
                               row_gate.reshape(n_blocks, MOE_BLOCK)))
    out = jnp.zeros((T + 1, D), jnp.float32).at[row_tok].add(y.reshape(P, D))[:T]
    return out.reshape(B, S, D).astype(x.dtype)


def setup_inputs(seed: int = 0) -> dict:
    key = jax.random.key(seed)
    ks = jax.random.split(key, 18)
    f32 = jnp.float32
    nrm = lambda k, shape, s: jax.random.normal(k, shape, f32) * s
    return {
        "x": nrm(ks[0], (BATCH, SEQ, D_MODEL), 1.0),
        "attn_norm_w": 1.0 + nrm(ks[1], (DEPTH, D_MODEL), 0.02),
        "w_in": nrm(ks[2], (DEPTH, D_MODEL, IN_WIDTH), D_MODEL ** -0.5),
        "q_norm_w": 1.0 + nrm(ks[3], (DEPTH, ATTN_HEAD_DIM), 0.02),
        "k_norm_w": 1.0 + nrm(ks[4], (DEPTH, ATTN_HEAD_DIM), 0.02),
        "attn_sinks": nrm(ks[5], (DEPTH, ATTN_HEADS), 0.5),
        "conv_w": nrm(ks[6], (DEPTH, CONV_WIDTH, 3 * DN_W), CONV_WIDTH ** -0.5),
        "a_log": jnp.log(jax.random.uniform(ks[7], (DEPTH, DN_HEADS), f32, 1.0, 16.0)),
        "dt_bias": 1.0 + nrm(ks[8], (DEPTH, DN_HEADS), 0.1),
        "dn_norm_w": 1.0 + nrm(ks[9], (DEPTH, DN_HEAD_DIM), 0.02),
        "w_out": nrm(ks[10], (DEPTH, MIX_WIDTH, D_MODEL), MIX_WIDTH ** -0.5),
        "ffn_norm_w": 1.0 + nrm(ks[11], (DEPTH, D_MODEL), 0.02),
        "w_router": nrm(ks[12], (DEPTH, D_MODEL, N_EXPERTS), D_MODEL ** -0.5),
        "b_router": nrm(ks[13], (DEPTH, N_EXPERTS), 0.01),
        "w_gate_up": nrm(ks[14], (DEPTH, N_EXPERTS, D_MODEL, 2 * D_EXPERT), D_MODEL ** -0.5),
        "b_gate_up": nrm(ks[15], (DEPTH, N_EXPERTS, 2 * D_EXPERT), 0.01),
        "w_down": nrm(ks[16], (DEPTH, N_EXPERTS, D_EXPERT, D_MODEL), D_EXPERT ** -0.5),
        "b_down": nrm(ks[17], (DEPTH, N_EXPERTS, D_MODEL), 0.01),
    }


def reference(x, attn_norm_w, w_in, q_norm_w, k_norm_w, attn_sinks, conv_w, a_log, dt_bias,
              dn_norm_w, w_out, ffn_norm_w, w_router, b_router, w_gate_up, b_gate_up,
              w_down, b_down):
    B, S = x.shape[0], x.shape[1]
    split_points = [int(p) for p in np.cumsum(IN_WIDTHS)[:-1]]
    h = x
    for l in range(DEPTH):
        u = rms_norm(h, attn_norm_w[l], NORM_EPS)
        proj = jnp.matmul(u, w_in[l])
        aq, ak, av, dq, dk, dv, dz, db, da = jnp.split(proj, split_points, axis=-1)
        aq = rms_norm(aq.reshape(B, S, ATTN_HEADS, ATTN_HEAD_DIM), q_norm_w[l], QK_NORM_EPS)
        ak = rms_norm(ak.reshape(B, S, ATTN_KV_HEADS, ATTN_HEAD_DIM), k_norm_w[l], QK_NORM_EPS)
        av = av.reshape(B, S, ATTN_KV_HEADS, ATTN_HEAD_DIM)
        attn_out = sliding_window_attention(aq, ak, av, attn_sinks[l])
        dn_out = gated_deltanet(dq, dk, dv, dz, db, da, conv_w[l], a_log[l], dt_bias[l],
                                dn_norm_w[l])
        mixed = jnp.concatenate([attn_out, dn_out.astype(attn_out.dtype)], axis=-1)
        h = h + jnp.matmul(mixed, w_out[l]).astype(h.dtype)
        u = rms_norm(h, ffn_norm_w[l], NORM_EPS)
        h = h + moe_ffn(u, w_router[l], b_router[l], w_gate_up[l], b_gate_up[l],
                        w_down[l], b_down[l])
    return h
```

```python
import functools
import math

import jax
import jax.numpy as jnp
from jax import lax
from jax.experimental import pallas as pl
from jax.experimental.pallas import tpu as pltpu

F32 = jnp.float32
BF16 = jnp.bfloat16

D_MODEL = 2048
ATTN_HEAD_DIM = 64
ATTN_HEADS = 16
ATTN_KV_HEADS = 4
ATTN_GROUP = ATTN_HEADS // ATTN_KV_HEADS
WINDOW = 128
DN_HEAD_DIM = 128
DN_HEADS = 8
DN_CHUNK = 64
CONV_WIDTH = 4
ATTN_Q_W = ATTN_HEADS * ATTN_HEAD_DIM
ATTN_KV_W = ATTN_KV_HEADS * ATTN_HEAD_DIM
DN_W = DN_HEADS * DN_HEAD_DIM
IN_WIDTH = ATTN_Q_W + 2 * ATTN_KV_W + 4 * DN_W + 2 * DN_HEADS
N_EXPERTS = 32
TOP_K = 4
D_EXPERT = 2048
SWIGLU_ALPHA = 1.702
SWIGLU_LIMIT = 7.0
NORM_EPS = 1e-5
QK_NORM_EPS = 1e-6
DN_NORM_EPS = 1e-6
L2_EPS = 1e-6

LANES = 128
MIB = 1024 * 1024
HIGHEST = lax.Precision.HIGHEST

ATTN_K_BLK = ATTN_Q_W // ATTN_KV_W
ATTN_V_BLK = ATTN_K_BLK + 1
DN_Q_BLK = (ATTN_Q_W + 2 * ATTN_KV_W) // DN_HEAD_DIM
DN_K_BLK = DN_Q_BLK + DN_HEADS
DN_V_BLK = DN_K_BLK + DN_HEADS
DN_Z_BLK = DN_V_BLK + DN_HEADS
DN_BA_BLK = DN_Z_BLK + DN_HEADS

MOE_ROWS = 256
MOE_ITEM_BLOCKS = 9
MOE_F_TILE = 256


def _dot(a, b, precision=None):
    return jnp.dot(a, b, preferred_element_type=F32, precision=precision)


def _dot_nt(a, b, precision=None):
    return lax.dot_general(a, b, (((1,), (1,)), ((), ())), preferred_element_type=F32,
                           precision=precision)


def _dot_tn(a, b, precision=None):
    return lax.dot_general(a, b, (((0,), (0,)), ((), ())), preferred_element_type=F32,
                           precision=precision)


def _inproj_kernel(x_ref, nw_ref, w_ref, o_ref, u_ref):
    @pl.when(pl.program_id(1) == 0)
    def _():
        x = x_ref[...]
        ms = jnp.mean(x * x, axis=-1, keepdims=True)
        u_ref[...] = (x * lax.rsqrt(ms + NORM_EPS) * nw_ref[...]).astype(BF16)

    o_ref[...] = _dot(u_ref[...], w_ref[...].astype(BF16))


def _inproj(x2d, norm_w, w_in, tm, tn):
    T, D = x2d.shape
    N = w_in.shape[1]
    return pl.pallas_call(
        _inproj_kernel,
        out_shape=jax.ShapeDtypeStruct((T, N), F32),
        grid=(T // tm, pl.cdiv(N, tn)),
        in_specs=[
            pl.BlockSpec((tm, D), lambda i, j: (i, 0)),
            pl.BlockSpec((1, D), lambda i, j: (0, 0)),
            pl.BlockSpec((D, tn), lambda i, j: (0, j)),
        ],
        out_specs=pl.BlockSpec((tm, tn), lambda i, j: (i, j)),
        scratch_shapes=[pltpu.VMEM((tm, D), BF16)],
        compiler_params=pltpu.CompilerParams(
            dimension_semantics=("arbitrary", "arbitrary"), vmem_limit_bytes=56 * MIB),
        name="inproj",
    )(x2d, norm_w.reshape(1, D), w_in)


def _attn_kernel(sink_ref, q_ref, kp_ref, kc_ref, vp_ref, vc_ref, qw_ref, kw_ref, o_ref):
    n = pl.program_id(1)
    L = WINDOW
    d = ATTN_HEAD_DIM

    def head_norm(t, w):
        ms = jnp.mean(t * t, axis=-1, keepdims=True)
        return t * lax.rsqrt(ms + QK_NORM_EPS) * w

    qi = lax.broadcasted_iota(jnp.int32, (L, 2 * L), 0)
    kj = lax.broadcasted_iota(jnp.int32, (L, 2 * L), 1)
    dist = qi + L - kj
    valid = (dist >= 0) & (dist < WINDOW) & ((kj >= L) | (n > 0))
    dist_f = dist.astype(F32)

    qw = qw_ref[...]
    kw = kw_ref[...]
    for kv in range(ATTN_KV_HEADS):
        cs = slice(kv * d, (kv + 1) * d)
        k = jnp.concatenate([kp_ref[:, cs], kc_ref[:, cs]], axis=0)
        k = head_norm(k, kw).astype(BF16)
        v = jnp.concatenate([vp_ref[:, cs], vc_ref[:, cs]], axis=0).astype(BF16)
        for g in range(ATTN_GROUP):
            hq = kv * ATTN_GROUP + g
            qs = slice(hq * d, (hq + 1) * d)
            q = head_norm(q_ref[:, qs], qw).astype(BF16)
            s = _dot_nt(q, k) * (d ** -0.5)
            slope = 2.0 ** (-8.0 * (hq + 1) / ATTN_HEADS)
            s = jnp.where(valid, s - slope * dist_f, -jnp.inf)
            sink = sink_ref[hq]
            m = jnp.maximum(jnp.max(s, axis=-1, keepdims=True), sink)
            p = jnp.exp(s - m)
            denom = jnp.sum(p, axis=-1, keepdims=True) + jnp.exp(sink - m)
            probs = (p / denom).astype(BF16)
            o_ref[:, qs] = _dot(probs, v).astype(o_ref.dtype)


def _attention(proj, sinks, q_norm_w, k_norm_w, B, S):
    nb = S // WINDOW
    L = WINDOW
    row = lambda b, n: b * nb + n
    prev = lambda b, n: b * nb + jnp.maximum(n - 1, 0)
    return pl.pallas_call(
        _attn_kernel,
        out_shape=jax.ShapeDtypeStruct((B * S, ATTN_Q_W), BF16),
        grid=(B, nb),
        in_specs=[
            pl.BlockSpec(memory_space=pltpu.SMEM),
            pl.BlockSpec((L, ATTN_Q_W), lambda b, n: (row(b, n), 0)),
            pl.BlockSpec((L, ATTN_KV_W), lambda b, n: (prev(b, n), ATTN_K_BLK)),
            pl.BlockSpec((L, ATTN_KV_W), lambda b, n: (row(b, n), ATTN_K_BLK)),
            pl.BlockSpec((L, ATTN_KV_W), lambda b, n: (prev(b, n), ATTN_V_BLK)),
            pl.BlockSpec((L, ATTN_KV_W), lambda b, n: (row(b, n), ATTN_V_BLK)),
            pl.BlockSpec((1, ATTN_HEAD_DIM), lambda b, n: (0, 0)),
            pl.BlockSpec((1, ATTN_HEAD_DIM), lambda b, n: (0, 0)),
        ],
        out_specs=pl.BlockSpec((L, ATTN_Q_W), lambda b, n: (row(b, n), 0)),
        compiler_params=pltpu.CompilerParams(dimension_semantics=("arbitrary", "arbitrary")),
        name="attn",
    )(sinks, proj, proj, proj, proj, proj, q_norm_w.reshape(1, -1), k_norm_w.reshape(1, -1))


CONV_PAD = 8


def _dn_kernel(alog_ref, dtb_ref, q_ref, k_ref, v_ref, z_ref, ba_ref, cq_ref, ck_ref, cv_ref,
               nw_ref, o_ref, xp, qs, ks, vs, gs, bs):
    h = pl.program_id(1)
    S = q_ref.shape[0]
    C = DN_CHUNK
    dk = DN_HEAD_DIM
    piece = min(S, 256)

    def conv_silu(x_ref, cw_ref, dst, l2):
        xp[0:CONV_PAD, :] = jnp.zeros((CONV_PAD, dk), F32)
        xp[CONV_PAD:CONV_PAD + S, :] = x_ref[...]
        for p in range(S // piece):
            r0 = p * piece
            acc = jnp.zeros((piece, dk), F32)
            for i in range(CONV_WIDTH):
                off = CONV_PAD + r0 - (CONV_WIDTH - 1) + i
                acc = acc + xp[off:off + piece, :] * cw_ref[i:i + 1, :]
            y = acc * jax.nn.sigmoid(acc)
            if l2:
                y = y * lax.rsqrt(jnp.sum(y * y, axis=-1, keepdims=True) + L2_EPS)
            dst[r0:r0 + piece, :] = y

    conv_silu(q_ref, cq_ref, qs, True)
    conv_silu(k_ref, ck_ref, ks, True)
    conv_silu(v_ref, cv_ref, vs, False)

    neg_a = -jnp.exp(jnp.full((1, dk), alog_ref[h], F32))
    dtb = dtb_ref[h]
    for p in range(S // piece):
        r0 = p * piece
        ba = ba_ref[r0:r0 + piece, :]
        lane = lax.broadcasted_iota(jnp.int32, ba.shape, 1)
        b_col = jnp.sum(jnp.where(lane == h, ba, 0.0), axis=-1, keepdims=True)
        a_col = jnp.sum(jnp.where(lane == DN_HEADS + h, ba, 0.0), axis=-1, keepdims=True)
        bs[r0:r0 + piece, :] = jnp.broadcast_to(jax.nn.sigmoid(b_col), (piece, dk))
        gs[r0:r0 + piece, :] = neg_a * jnp.broadcast_to(jax.nn.softplus(a_col + dtb), (piece, dk))

    ri = lax.broadcasted_iota(jnp.int32, (C, C), 0)
    ci = lax.broadcasted_iota(jnp.int32, (C, C), 1)
    incl = ri >= ci
    strict = ri > ci
    tri_ones = jnp.where(incl, 1.0, 0.0).astype(F32)
    upper_ones = jnp.where(ri <= ci, 1.0, 0.0).astype(F32)
    nw = nw_ref[...]

    def chunk(c, state):
        r = pl.multiple_of(c * C, C)
        rows = pl.ds(r, C)
        q = qs[rows, :] * (dk ** -0.5)
        k = ks[rows, :]
        v = vs[rows, :]
        g = gs[rows, :]
        beta = bs[rows, :]
        gc = _dot(tri_ones, g, HIGHEST)
        gc_row = _dot_tn(g[:, :C], upper_ones, HIGHEST)
        decay = jnp.exp(jnp.where(incl, gc[:, :C] - gc_row, -jnp.inf))
        kb = k * beta
        vb = v * beta
        m = -jnp.where(strict, _dot_nt(kb, k, HIGHEST) * decay, 0.0)
        x = jnp.concatenate([vb, kb * jnp.exp(gc)], axis=1)
        n_fac = int(math.log2(C))
        for it in range(n_fac):
            x = x + _dot(m, x, HIGHEST)
            if it + 1 < n_fac:
                m = _dot(m, m, HIGHEST)
        u = x[:, :dk]
        w = x[:, dk:]
        a_intra = jnp.where(incl, _dot_nt(q, k, HIGHEST) * decay, 0.0)
        v_new = u - _dot(w, state, HIGHEST)
        o = _dot(q * jnp.exp(gc), state, HIGHEST) + _dot(a_intra, v_new, HIGHEST)
        g_last = gc[C - 1:C, :]
        k_dec = k * jnp.exp(g_last - gc)
        state = state * jnp.exp(g_last) + _dot_tn(k_dec, v_new, HIGHEST)
        z = z_ref[rows, :]
        o = o * lax.rsqrt(jnp.mean(o * o, axis=-1, keepdims=True) + DN_NORM_EPS) * nw
        o_ref[rows, :] = (o * (z * jax.nn.sigmoid(z))).astype(o_ref.dtype)
        return state

    lax.fori_loop(0, S // C, chunk, jnp.zeros((dk, dk), F32))


def _deltanet(proj, conv_w, a_log, dt_bias, dn_norm_w, B, S):
    dk = DN_HEAD_DIM
    seq = lambda blk: pl.BlockSpec((S, dk), lambda b, h: (b, blk + h))
    cw = lambda blk: pl.BlockSpec((CONV_WIDTH, dk), lambda b, h: (0, blk + h))
    return pl.pallas_call(
        _dn_kernel,
        out_shape=jax.ShapeDtypeStruct((B * S, DN_W), BF16),
        grid=(B, DN_HEADS),
        in_specs=[
            pl.BlockSpec(memory_space=pltpu.SMEM),
            pl.BlockSpec(memory_space=pltpu.SMEM),
            seq(DN_Q_BLK), seq(DN_K_BLK), seq(DN_V_BLK), seq(DN_Z_BLK),
            pl.BlockSpec((S, LANES), lambda b, h: (b, DN_BA_BLK)),
            cw(0), cw(DN_HEADS), cw(2 * DN_HEADS),
            pl.BlockSpec((1, dk), lambda b, h: (0, 0)),
        ],
        out_specs=pl.BlockSpec((S, dk), lambda b, h: (b, h)),
        scratch_shapes=[pltpu.VMEM((S + CONV_PAD, dk), F32)] + [pltpu.VMEM((S, dk), F32)] * 5,
        compiler_params=pltpu.CompilerParams(
            dimension_semantics=("arbitrary", "arbitrary"), vmem_limit_bytes=40 * MIB),
        name="deltanet",
    )(a_log, dt_bias, proj, proj, proj, proj, proj, conv_w, conv_w, conv_w,
      dn_norm_w.reshape(1, dk))


def _outproj_kernel(a_ref, d_ref, x_ref, wa_ref, wd_ref, nw_ref, wr_ref, br_ref,
                    h_ref, u_ref, ti_ref, tg_ref):
    h = x_ref[...] + _dot(a_ref[...], wa_ref[...]) + _dot(d_ref[...], wd_ref[...])
    h_ref[...] = h
    ms = jnp.mean(h * h, axis=-1, keepdims=True)
    u = h * lax.rsqrt(ms + NORM_EPS) * nw_ref[...]
    u_ref[...] = u
    logits = _dot(u, wr_ref[...], HIGHEST) + br_ref[...]
    lane = lax.broadcasted_iota(jnp.int32, logits.shape, 1)
    lane_f = lane.astype(F32)
    vals = jnp.where(lane < N_EXPERTS, logits, -jnp.inf)
    top_v, top_i = [], []
    for _ in range(TOP_K):
        m = jnp.max(vals, axis=-1, keepdims=True)
        idx = jnp.min(jnp.where(vals == m, lane_f, float(LANES)), axis=-1, keepdims=True)
        top_v.append(m)
        top_i.append(idx)
        vals = jnp.where(lane_f == idx, -jnp.inf, vals)
    e = [jnp.exp(v - top_v[0]) for v in top_v]
    denom = e[0] + e[1] + e[2] + e[3]
    ti = jnp.zeros(logits.shape, F32)
    tg = jnp.zeros(logits.shape, F32)
    for k in range(TOP_K):
        ti = jnp.where(lane == k, top_i[k], ti)
        tg = jnp.where(lane == k, e[k] / denom, tg)
    ti_ref[...] = ti.astype(jnp.int32)
    tg_ref[...] = tg


def _outproj_router(attn_o, dn_o, x2d, w_out, ffn_norm_w, w_router, b_router, tm):
    T, D = x2d.shape
    wa = w_out[:ATTN_Q_W].astype(BF16)
    wd = w_out[ATTN_Q_W:].astype(BF16)
    wr = jnp.zeros((D, LANES), F32).at[:, :N_EXPERTS].set(w_router)
    br = jnp.zeros((1, LANES), F32).at[0, :N_EXPERTS].set(b_router)
    const = lambda shape: pl.BlockSpec(shape, lambda i: (0, 0))
    tile = lambda w: pl.BlockSpec((tm, w), lambda i: (i, 0))
    return pl.pallas_call(
        _outproj_kernel,
        out_shape=(jax.ShapeDtypeStruct((T, D), F32), jax.ShapeDtypeStruct((T, D), F32),
                   jax.ShapeDtypeStruct((T, LANES), jnp.int32),
                   jax.ShapeDtypeStruct((T, LANES), F32)),
        grid=(T // tm,),
        in_specs=[tile(ATTN_Q_W), tile(DN_W), tile(D), const((ATTN_Q_W, D)), const((DN_W, D)),
                  const((1, D)), const((D, LANES)), const((1, LANES))],
        out_specs=(tile(D), tile(D), tile(LANES), tile(LANES)),
        compiler_params=pltpu.CompilerParams(
            dimension_semantics=("arbitrary",), vmem_limit_bytes=56 * MIB),
        name="outproj",
    )(attn_o, dn_o, x2d, wa, wd, ffn_norm_w.reshape(1, D), wr, br)


def _gather_kernel(tok_ref, nblk_ref, u_ref, o_ref, buf, sem):
    i = pl.program_id(0)
    R = buf.shape[0]

    @pl.when(i < nblk_ref[0])
    def _():
        def issue(r, _):
            tok = tok_ref[i * R + r]
            pltpu.make_async_copy(u_ref.at[pl.ds(tok, 1), :], buf.at[pl.ds(r, 1), :], sem).start()
            return 0

        lax.fori_loop(0, R, issue, 0)
        pltpu.make_async_copy(u_ref.at[pl.ds(0, R), :], buf, sem).wait()
        o_ref[...] = buf[...].astype(o_ref.dtype)

    @pl.when(i >= nblk_ref[0])
    def _():
        o_ref[...] = jnp.zeros(o_ref.shape, o_ref.dtype)


def _gather_rows(u, row_tok, n_blocks_used, P):
    T, D = u.shape
    R = MOE_ROWS
    return pl.pallas_call(
        _gather_kernel,
        out_shape=jax.ShapeDtypeStruct((P, D), BF16),
        grid_spec=pltpu.PrefetchScalarGridSpec(
            num_scalar_prefetch=2,
            grid=(P // R,),
            in_specs=[pl.BlockSpec(memory_space=pl.ANY)],
            out_specs=pl.BlockSpec((R, D), lambda i, tok, nb: (i, 0)),
            scratch_shapes=[pltpu.VMEM((R, D), F32), pltpu.SemaphoreType.DMA],
        ),
        compiler_params=pltpu.CompilerParams(dimension_semantics=("arbitrary",)),
        name="gather",
    )(row_tok, n_blocks_used, u)


def _experts_kernel(ie_ref, ib_ref, in_ref, used_ref, x_ref, wg_ref, wl_ref, wd_ref, bg_ref, bl_ref, bd_ref,
                    y_ref, xbuf, ybuf, wg_s, wl_s, wd_s, sem):
    w = pl.program_id(0)
    j = pl.program_id(1)
    nj = pl.num_programs(1)
    R = MOE_ROWS
    nb = in_ref[w]
    row0 = ib_ref[w] * R

    def x_copy(r):
        return pltpu.make_async_copy(x_ref.at[pl.ds(row0 + r * R, R), :],
                                     xbuf.at[pl.ds(r * R, R), :], sem)

    def y_copy(r):
        return pltpu.make_async_copy(ybuf.at[pl.ds(r * R, R), :],
                                     y_ref.at[pl.ds(row0 + r * R, R), :], sem)

    def for_blocks(fn):
        def body(r, _):
            fn(r)
            return 0
        lax.fori_loop(0, nb, body, 0)

    def rows_of(r):
        return pl.ds(pl.multiple_of(r * R, R), R)

    @pl.when(j == 0)
    def _():
        for_blocks(lambda r: x_copy(r).start())
        bd = jnp.broadcast_to(bd_ref[...], (R, ybuf.shape[1]))

        def init(r):
            ybuf[rows_of(r), :] = bd

        for_blocks(init)
        for_blocks(lambda r: x_copy(r).wait())

    @pl.when(nb > 0)
    def _():
        wg_s[...] = wg_ref[...].astype(BF16)
        wl_s[...] = wl_ref[...].astype(BF16)
        wd_s[...] = wd_ref[...].astype(BF16)

    bg = bg_ref[...]
    bl = bl_ref[...]

    def block(r):
        rows = rows_of(r)
        x = xbuf[rows, :]
        hg = jnp.minimum(_dot(x, wg_s[...]) + bg, SWIGLU_LIMIT)
        hl = jnp.clip(_dot(x, wl_s[...]) + bl, -SWIGLU_LIMIT, SWIGLU_LIMIT)
        act = hg * jax.nn.sigmoid(SWIGLU_ALPHA * hg) * (hl + 1.0)
        ybuf[rows, :] += _dot(act.astype(BF16), wd_s[...])

    for_blocks(block)

    @pl.when(j == nj - 1)
    def _():
        for_blocks(lambda r: y_copy(r).start())
        for_blocks(lambda r: y_copy(r).wait())

    @pl.when((j == nj - 1) & (w == pl.num_programs(0) - 1))
    def _():
        ybuf[0:R, :] = jnp.zeros((R, ybuf.shape[1]), F32)

        def tail_copy(r):
            return pltpu.make_async_copy(ybuf.at[pl.ds(0, R), :], y_ref.at[pl.ds(r * R, R), :], sem)

        def tail(fn):
            def body(r, _):
                fn(r)
                return 0
            lax.fori_loop(used_ref[0], y_ref.shape[0] // R, body, 0)

        tail(lambda r: tail_copy(r).start())
        tail(lambda r: tail_copy(r).wait())


def _experts(x_sorted, item_expert, item_blk0, item_nblk, n_blocks_used, w_gate_up, b_gate_up, w_down, b_down):
    P, D = x_sorted.shape
    E, _, F2 = w_gate_up.shape
    F = F2 // 2
    tf = MOE_F_TILE
    nj = F // tf
    n_items = item_expert.shape[0]
    rows = MOE_ITEM_BLOCKS * MOE_ROWS
    b_gu = b_gate_up.reshape(E, 1, F2)
    b_d = b_down.reshape(E, 1, D)
    return pl.pallas_call(
        _experts_kernel,
        out_shape=jax.ShapeDtypeStruct((P, D), F32),
        grid_spec=pltpu.PrefetchScalarGridSpec(
            num_scalar_prefetch=4,
            grid=(n_items, nj),
            in_specs=[
                pl.BlockSpec(memory_space=pl.ANY),
                pl.BlockSpec((None, D, tf), lambda w, j, ie, *_: (ie[w], 0, j)),
                pl.BlockSpec((None, D, tf), lambda w, j, ie, *_: (ie[w], 0, nj + j)),
                pl.BlockSpec((None, tf, D), lambda w, j, ie, *_: (ie[w], j, 0)),
                pl.BlockSpec((None, 1, tf), lambda w, j, ie, *_: (ie[w], 0, j)),
                pl.BlockSpec((None, 1, tf), lambda w, j, ie, *_: (ie[w], 0, nj + j)),
                pl.BlockSpec((None, 1, D), lambda w, j, ie, *_: (ie[w], 0, 0)),
            ],
            out_specs=pl.BlockSpec(memory_space=pl.ANY),
            scratch_shapes=[
                pltpu.VMEM((rows, D), BF16), pltpu.VMEM((rows, D), F32),
                pltpu.VMEM((D, tf), BF16), pltpu.VMEM((D, tf), BF16), pltpu.VMEM((tf, D), BF16),
                pltpu.SemaphoreType.DMA,
            ],
        ),
        compiler_params=pltpu.CompilerParams(
            dimension_semantics=("arbitrary", "arbitrary"), vmem_limit_bytes=58 * MIB),
        name="experts",
    )(item_expert, item_blk0, item_nblk, n_blocks_used, x_sorted, w_gate_up, w_gate_up, w_down, b_gu, b_gu, b_d)


def _combine_kernel(dest_ref, h_ref, tg_ref, y_ref, o_ref, buf, sem):
    i = pl.program_id(0)
    tm = h_ref.shape[0]

    def issue(r, _):
        for k in range(TOP_K):
            row = dest_ref[(i * tm + r) * TOP_K + k]
            pltpu.make_async_copy(y_ref.at[pl.ds(row, 1), :], buf.at[k, pl.ds(r, 1), :], sem).start()
        return 0

    lax.fori_loop(0, tm, issue, 0)
    for k in range(TOP_K):
        pltpu.make_async_copy(y_ref.at[pl.ds(0, tm), :], buf.at[k], sem).wait()
    tg = tg_ref[...]
    acc = h_ref[...]
    for k in range(TOP_K):
        acc = acc + tg[:, k:k + 1] * buf[k]
    o_ref[...] = acc


def _combine(h, top_g, y_sorted, dest, tm):
    T, D = h.shape
    return pl.pallas_call(
        _combine_kernel,
        out_shape=jax.ShapeDtypeStruct((T, D), F32),
        grid_spec=pltpu.PrefetchScalarGridSpec(
            num_scalar_prefetch=1,
            grid=(T // tm,),
            in_specs=[
                pl.BlockSpec((tm, D), lambda i, d: (i, 0)),
                pl.BlockSpec((tm, LANES), lambda i, d: (i, 0)),
                pl.BlockSpec(memory_space=pl.ANY),
            ],
            out_specs=pl.BlockSpec((tm, D), lambda i, d: (i, 0)),
            scratch_shapes=[pltpu.VMEM((TOP_K, tm, D), F32), pltpu.SemaphoreType.DMA],
        ),
        compiler_params=pltpu.CompilerParams(dimension_semantics=("arbitrary",)),
        name="combine",
    )(dest, h, top_g, y_sorted)


def _routing(top_i, T):
    R = MOE_ROWS
    n_assign = T * TOP_K
    e_flat = top_i[:, :TOP_K].reshape(n_assign)
    onehot = (e_flat[:, None] == jnp.arange(N_EXPERTS, dtype=jnp.int32)[None, :]).astype(jnp.int32)
    csum = jnp.cumsum(onehot, axis=0)
    rank = jnp.take_along_axis(csum, e_flat[:, None], axis=1)[:, 0] - 1
    counts = csum[-1]
    nblk = (counts + R - 1) // R
    blk_end = jnp.cumsum(nblk)
    blk_start = blk_end - nblk
    dest = blk_start[e_flat] * R + rank
    max_blocks = n_assign // R + N_EXPERTS
    P = max_blocks * R
    tok = jnp.arange(n_assign, dtype=jnp.int32) // TOP_K
    row_tok = jnp.zeros((P,), jnp.int32).at[dest].set(tok)
    nb_item = MOE_ITEM_BLOCKS
    n_items_e = (nblk + nb_item - 1) // nb_item
    item_end = jnp.cumsum(n_items_e)
    item_start = item_end - n_items_e
    max_items = N_EXPERTS + -(-max_blocks // nb_item)
    w = jnp.arange(max_items, dtype=jnp.int32)
    ie = jnp.minimum(jnp.searchsorted(item_end, w, side='right'), N_EXPERTS - 1).astype(jnp.int32)
    local = w - item_start[ie]
    live = w < item_end[-1]
    ib = blk_start[ie] + local * nb_item
    inb = jnp.clip(nblk[ie] - local * nb_item, 0, nb_item)
    last_e = ie[jnp.maximum(item_end[-1] - 1, 0)]
    ie = jnp.where(live, ie, last_e).astype(jnp.int32)
    ib = jnp.where(live, ib, 0).astype(jnp.int32)
    inb = jnp.where(live, inb, 0).astype(jnp.int32)
    return dest.astype(jnp.int32), row_tok, blk_end[-1:].astype(jnp.int32), P, ie, ib, inb


def _layer(x, attn_norm_w, w_in, q_norm_w, k_norm_w, attn_sinks, conv_w, a_log, dt_bias,
           dn_norm_w, w_out, ffn_norm_w, w_router, b_router, w_gate_up, b_gate_up, w_down, b_down):
    B, S, D = x.shape
    T = B * S
    x2d = x.reshape(T, D)
    proj = _inproj(x2d, attn_norm_w, w_in, tm=min(T, 1024), tn=512)
    attn_o = _attention(proj, attn_sinks, q_norm_w, k_norm_w, B, S)
    dn_o = _deltanet(proj, conv_w, a_log, dt_bias, dn_norm_w, B, S)
    h, u, top_i, top_g = _outproj_router(attn_o, dn_o, x2d, w_out, ffn_norm_w, w_router, b_router,
                                         tm=min(T, 512))
    dest, row_tok, n_blocks_used, P, ie, ib, inb = _routing(top_i, T)
    x_sorted = _gather_rows(u, row_tok, n_blocks_used, P)
    y_sorted = _experts(x_sorted, ie, ib, inb, n_blocks_used, w_gate_up, b_gate_up, w_down, b_down)
    out = _combine(h, top_g, y_sorted, dest, tm=min(T, 128))
    return out.reshape(B, S, D)


def kernel(x, attn_norm_w, w_in, q_norm_w, k_norm_w, attn_sinks, conv_w, a_log, dt_bias, dn_norm_w,
           w_out, ffn_norm_w, w_router, b_router, w_gate_up, b_gate_up, w_down, b_down):
    h = x
    for l in range(attn_norm_w.shape[0]):
        h = _layer(h, attn_norm_w[l], w_in[l], q_norm_w[l], k_norm_w[l], attn_sinks[l], conv_w[l],
                   a_log[l], dt_bias[l], dn_norm_w[l], w_out[l], ffn_norm_w[l], w_router[l],
                   b_router[l], w_gate_up[l], b_gate_up[l], w_down[l], b_down[l])
    return h
```

```python
import functools
import math

import jax
import jax.numpy as jnp
from jax import lax
from jax.experimental import pallas as pl
from jax.experimental.pallas import tpu as pltpu

F32 = jnp.float32
BF16 = jnp.bfloat16

D_MODEL = 2048
ATTN_HEAD_DIM = 64
ATTN_HEADS = 16
ATTN_KV_HEADS = 4
ATTN_GROUP = ATTN_HEADS // ATTN_KV_HEADS
WINDOW = 128
DN_HEAD_DIM = 128
DN_HEADS = 8
DN_CHUNK = 64
CONV_WIDTH = 4
ATTN_Q_W = ATTN_HEADS * ATTN_HEAD_DIM
ATTN_KV_W = ATTN_KV_HEADS * ATTN_HEAD_DIM
DN_W = DN_HEADS * DN_HEAD_DIM
IN_WIDTH = ATTN_Q_W + 2 * ATTN_KV_W + 4 * DN_W + 2 * DN_HEADS
N_EXPERTS = 32
TOP_K = 4
D_EXPERT = 2048
SWIGLU_ALPHA = 1.702
SWIGLU_LIMIT = 7.0
NORM_EPS = 1e-5
QK_NORM_EPS = 1e-6
DN_NORM_EPS = 1e-6
L2_EPS = 1e-6

LANES = 128
MIB = 1024 * 1024
HIGHEST = lax.Precision.HIGHEST

ATTN_K_BLK = ATTN_Q_W // ATTN_KV_W
ATTN_V_BLK = ATTN_K_BLK + 1
DN_Q_BLK = (ATTN_Q_W + 2 * ATTN_KV_W) // DN_HEAD_DIM
DN_K_BLK = DN_Q_BLK + DN_HEADS
DN_V_BLK = DN_K_BLK + DN_HEADS
DN_Z_BLK = DN_V_BLK + DN_HEADS
DN_BA_BLK = DN_Z_BLK + DN_HEADS

MOE_ROWS = 256
MOE_ITEM_BLOCKS = 9
MOE_F_TILE = 256


def _dot(a, b, precision=None):
    return jnp.dot(a, b, preferred_element_type=F32, precision=precision)


def _dot_nt(a, b, precision=None):
    return lax.dot_general(a, b, (((1,), (1,)), ((), ())), preferred_element_type=F32,
                           precision=precision)


def _dot_tn(a, b, precision=None):
    return lax.dot_general(a, b, (((0,), (0,)), ((), ())), preferred_element_type=F32,
                           precision=precision)


def _inproj_kernel(x_ref, nw_ref, w_ref, o_ref, u_ref):
    @pl.when(pl.program_id(1) == 0)
    def _():
        x = x_ref[...]
        ms = jnp.mean(x * x, axis=-1, keepdims=True)
        u_ref[...] = (x * lax.rsqrt(ms + NORM_EPS) * nw_ref[...]).astype(BF16)

    o_ref[...] = _dot(u_ref[...], w_ref[...].astype(BF16))


def _inproj(x2d, norm_w, w_in, tm, tn):
    T, D = x2d.shape
    N = w_in.shape[1]
    return pl.pallas_call(
        _inproj_kernel,
        out_shape=jax.ShapeDtypeStruct((T, N), F32),
        grid=(T // tm, pl.cdiv(N, tn)),
        in_specs=[
            pl.BlockSpec((tm, D), lambda i, j: (i, 0)),
            pl.BlockSpec((1, D), lambda i, j: (0, 0)),
            pl.BlockSpec((D, tn), lambda i, j: (0, j)),
        ],
        out_specs=pl.BlockSpec((tm, tn), lambda i, j: (i, j)),
        scratch_shapes=[pltpu.VMEM((tm, D), BF16)],
        compiler_params=pltpu.CompilerParams(
            dimension_semantics=("arbitrary", "arbitrary"), vmem_limit_bytes=56 * MIB),
        name="inproj",
    )(x2d, norm_w.reshape(1, D), w_in)


def _attn_kernel(sink_ref, q_ref, kp_ref, kc_ref, vp_ref, vc_ref, qw_ref, kw_ref, o_ref):
    n = pl.program_id(1)
    L = WINDOW
    d = ATTN_HEAD_DIM

    def head_norm(t, w):
        ms = jnp.mean(t * t, axis=-1, keepdims=True)
        return t * lax.rsqrt(ms + QK_NORM_EPS) * w

    qi = lax.broadcasted_iota(jnp.int32, (L, 2 * L), 0)
    kj = lax.broadcasted_iota(jnp.int32, (L, 2 * L), 1)
    dist = qi + L - kj
    valid = (dist >= 0) & (dist < WINDOW) & ((kj >= L) | (n > 0))
    dist_f = dist.astype(F32)

    qw = qw_ref[...]
    kw = kw_ref[...]
    for kv in range(ATTN_KV_HEADS):
        cs = slice(kv * d, (kv + 1) * d)
        k = jnp.concatenate([kp_ref[:, cs], kc_ref[:, cs]], axis=0)
        k = head_norm(k, kw).astype(BF16)
        v = jnp.concatenate([vp_ref[:, cs], vc_ref[:, cs]], axis=0).astype(BF16)
        for g in range(ATTN_GROUP):
            hq = kv * ATTN_GROUP + g
            qs = slice(hq * d, (hq + 1) * d)
            q = head_norm(q_ref[:, qs], qw).astype(BF16)
            s = _dot_nt(q, k) * (d ** -0.5)
            slope = 2.0 ** (-8.0 * (hq + 1) / ATTN_HEADS)
            s = jnp.where(valid, s - slope * dist_f, -jnp.inf)
            sink = sink_ref[hq]
            m = jnp.maximum(jnp.max(s, axis=-1, keepdims=True), sink)
            p = jnp.exp(s - m)
            denom = jnp.sum(p, axis=-1, keepdims=True) + jnp.exp(sink - m)
            probs = (p / denom).astype(BF16)
            o_ref[:, qs] = _dot(probs, v).astype(o_ref.dtype)


def _attention(proj, sinks, q_norm_w, k_norm_w, B, S):
    nb = S // WINDOW
    L = WINDOW
    row = lambda b, n: b * nb + n
    prev = lambda b, n: b * nb + jnp.maximum(n - 1, 0)
    return pl.pallas_call(
        _attn_kernel,
        out_shape=jax.ShapeDtypeStruct((B * S, ATTN_Q_W), BF16),
        grid=(B, nb),
        in_specs=[
            pl.BlockSpec(memory_space=pltpu.SMEM),
            pl.BlockSpec((L, ATTN_Q_W), lambda b, n: (row(b, n), 0)),
            pl.BlockSpec((L, ATTN_KV_W), lambda b, n: (prev(b, n), ATTN_K_BLK)),
            pl.BlockSpec((L, ATTN_KV_W), lambda b, n: (row(b, n), ATTN_K_BLK)),
            pl.BlockSpec((L, ATTN_KV_W), lambda b, n: (prev(b, n), ATTN_V_BLK)),
            pl.BlockSpec((L, ATTN_KV_W), lambda b, n: (row(b, n), ATTN_V_BLK)),
            pl.BlockSpec((1, ATTN_HEAD_DIM), lambda b, n: (0, 0)),
            pl.BlockSpec((1, ATTN_HEAD_DIM), lambda b, n: (0, 0)),
        ],
        out_specs=pl.BlockSpec((L, ATTN_Q_W), lambda b, n: (row(b, n), 0)),
        compiler_params=pltpu.CompilerParams(dimension_semantics=("arbitrary", "arbitrary")),
        name="attn",
    )(sinks, proj, proj, proj, proj, proj, q_norm_w.reshape(1, -1), k_norm_w.reshape(1, -1))


CONV_PAD = 8


DN_GROUP = 16

_X3 = ((0, 0), (1, 0), (0, 1))
_EXACT_LHS = ((0, 0), (0, 1), (0, 2))
_EXACT_RHS = ((0, 0), (1, 0), (2, 0))


def _split(x, n):
    parts = []
    for i in range(n):
        p = x.astype(BF16)
        parts.append(p)
        if i + 1 < n:
            x = x - p.astype(F32)
    return parts


def _mm(a_parts, b_parts, terms, form='nn'):
    a_axis = 0 if form == 'tn' else 1
    b_axis = 1 if form == 'nt' else 0
    a = jnp.concatenate([a_parts[i] for i, _ in terms], axis=a_axis)
    b = jnp.concatenate([b_parts[j] for _, j in terms], axis=b_axis)
    return {'nn': _dot, 'nt': _dot_nt, 'tn': _dot_tn}[form](a, b)


def _dn_kernel(alog_ref, dtb_ref, q_ref, k_ref, v_ref, z_ref, ba_ref, cq_ref, ck_ref, cv_ref,
               nw_ref, o_ref, xp, qs, ks, vs, gs, bs, us, ws, am, pm, qm, gl):
    h = pl.program_id(1)
    S = q_ref.shape[0]
    C = DN_CHUNK
    dk = DN_HEAD_DIM
    piece = min(S, 256)

    def conv_silu(x_ref, cw_ref, dst, l2):
        xp[0:CONV_PAD, :] = jnp.zeros((CONV_PAD, dk), F32)
        xp[CONV_PAD:CONV_PAD + S, :] = x_ref[...]
        for p in range(S // piece):
            r0 = p * piece
            acc = jnp.zeros((piece, dk), F32)
            for i in range(CONV_WIDTH):
                off = CONV_PAD + r0 - (CONV_WIDTH - 1) + i
                acc = acc + xp[off:off + piece, :] * cw_ref[i:i + 1, :]
            y = acc * jax.nn.sigmoid(acc)
            if l2:
                y = y * lax.rsqrt(jnp.sum(y * y, axis=-1, keepdims=True) + L2_EPS)
            dst[r0:r0 + piece, :] = y

    conv_silu(q_ref, cq_ref, qs, True)
    conv_silu(k_ref, ck_ref, ks, True)
    conv_silu(v_ref, cv_ref, vs, False)

    neg_a = -jnp.exp(jnp.full((1, dk), alog_ref[h], F32))
    dtb = dtb_ref[h]
    for p in range(S // piece):
        r0 = p * piece
        ba = ba_ref[r0:r0 + piece, :]
        lane = lax.broadcasted_iota(jnp.int32, ba.shape, 1)
        b_col = jnp.sum(jnp.where(lane == h, ba, 0.0), axis=-1, keepdims=True)
        a_col = jnp.sum(jnp.where(lane == DN_HEADS + h, ba, 0.0), axis=-1, keepdims=True)
        bs[r0:r0 + piece, :] = jnp.broadcast_to(jax.nn.sigmoid(b_col), (piece, dk))
        gs[r0:r0 + piece, :] = neg_a * jnp.broadcast_to(jax.nn.softplus(a_col + dtb), (piece, dk))

    ri = lax.broadcasted_iota(jnp.int32, (C, C), 0)
    ci = lax.broadcasted_iota(jnp.int32, (C, C), 1)
    incl = ri >= ci
    strict = ri > ci
    tri_ones = [jnp.where(incl, 1.0, 0.0).astype(BF16)]
    upper_ones = [jnp.where(ri <= ci, 1.0, 0.0).astype(BF16)]
    nw = nw_ref[...]
    n_chunks = S // C
    group = DN_GROUP if n_chunks % DN_GROUP == 0 else 1

    def intra_load(c):
        rows = pl.ds(pl.multiple_of(c * C, C), C)
        return qs[rows, :], ks[rows, :], vs[rows, :], gs[rows, :], bs[rows, :]

    def intra_compute(loaded):
        each = lambda f, *ls: [f(*a) for a in zip(*ls)]
        q, k, v, g, beta = (list(t) for t in zip(*loaded))
        q = each(lambda t: t * (dk ** -0.5), q)
        g3 = each(lambda t: _split(t, 3), g)
        gc = each(lambda p: _mm(tri_ones, p, _EXACT_LHS), g3)
        gc_row = each(lambda p: _mm([t[:, :C] for t in p], upper_ones, _EXACT_RHS, 'tn'), g3)
        decay = each(lambda a, b: jnp.exp(jnp.where(incl, a[:, :C] - b, -jnp.inf)), gc, gc_row)
        kb = each(jnp.multiply, k, beta)
        k2 = each(lambda t: _split(t, 2), k)
        kk = each(lambda a, b: _mm(_split(a, 2), b, _X3, 'nt'), kb, k2)
        qk = each(lambda a, b: _mm(_split(a, 2), b, _X3, 'nt'), q, k2)
        m = each(lambda a, d: -jnp.where(strict, a * d, 0.0), kk, decay)
        x = each(lambda vv, bb, kbb, gg: jnp.concatenate([vv * bb, kbb * jnp.exp(gg)], axis=1),
                 v, beta, kb, gc)
        n_fac = int(math.log2(C))
        for it in range(n_fac):
            m2 = each(lambda t: _split(t, 2), m)
            x = each(lambda xx, mm: xx + _mm(mm, _split(xx, 2), _X3), x, m2)
            if it + 1 < n_fac:
                m = each(lambda mm: _mm(mm, mm, _X3), m2)
        kd2 = each(lambda kk_, gg: _split(kk_ * jnp.exp(gg[C - 1:C, :] - gg), 2), k, gc)
        pq = each(lambda a, xx: _mm(a, _split(xx, 2), _X3, 'tn'), kd2, x)
        a_intra = each(lambda a, d: jnp.where(incl, a * d, 0.0), qk, decay)
        qg = each(lambda a, gg: a * jnp.exp(gg), q, gc)
        e_last = each(lambda gg: jnp.broadcast_to(jnp.exp(gg[C - 1:C, :]), (8, dk)), gc)
        return list(zip(x, qg, a_intra, pq, e_last))

    def intra_store(c, x, qg, a_intra, pq, e_last):
        rows = pl.ds(pl.multiple_of(c * C, C), C)
        prow = pl.ds(pl.multiple_of(c * dk, dk), dk)
        us[rows, :] = x[:, :dk]
        ws[rows, :] = x[:, dk:]
        qs[rows, :] = qg
        am[rows, :] = a_intra
        qm[prow, :] = pq[:, :dk]
        pm[prow, :] = pq[:, dk:]
        gl[pl.ds(pl.multiple_of(c * 8, 8), 8), :] = e_last

    def intra_group(i, _):
        loaded = [intra_load(i * group + t) for t in range(group)]
        outs = intra_compute(loaded)
        for t in range(group):
            intra_store(i * group + t, *outs[t])
        return 0

    lax.fori_loop(0, n_chunks // group, intra_group, 0)

    def scan(c, state):
        prow = pl.ds(pl.multiple_of(c * dk, dk), dk)
        ps = _mm(_split(pm[prow, :], 2), _split(state, 2), _X3)
        pm[prow, :] = state
        e_last = gl[pl.ds(pl.multiple_of(c * 8, 8), 1), :]
        return state * e_last - ps + qm[prow, :]

    lax.fori_loop(0, n_chunks, scan, jnp.zeros((dk, dk), F32), unroll=2)

    def out_group(i, _):
        cs = [i * group + t for t in range(group)]
        rows = [pl.ds(pl.multiple_of(c * C, C), C) for c in cs]
        each = lambda f, *ls: [f(*a) for a in zip(*ls)]
        state2 = each(lambda c: _split(pm[pl.ds(pl.multiple_of(c * dk, dk), dk), :], 2), cs)
        lhs2 = each(lambda r: _split(jnp.concatenate([ws[r, :], qs[r, :]], axis=0), 2), rows)
        prod = each(lambda a, b: _mm(a, b, _X3), lhs2, state2)
        v_new = each(lambda r, p: us[r, :] - p[:C], rows, prod)
        av = each(lambda r, vn: _mm(_split(am[r, :], 2), _split(vn, 2), _X3), rows, v_new)
        o = each(lambda p, a: p[C:] + a, prod, av)
        o = each(lambda t: t * lax.rsqrt(jnp.mean(t * t, axis=-1, keepdims=True) + DN_NORM_EPS) * nw, o)
        for r, t in zip(rows, o):
            z = z_ref[r, :]
            o_ref[r, :] = (t * (z * jax.nn.sigmoid(z))).astype(o_ref.dtype)
        return 0

    lax.fori_loop(0, n_chunks // group, out_group, 0)


def _deltanet(proj, conv_w, a_log, dt_bias, dn_norm_w, B, S):
    dk = DN_HEAD_DIM
    seq = lambda blk: pl.BlockSpec((S, dk), lambda b, h: (b, blk + h))
    cw = lambda blk: pl.BlockSpec((CONV_WIDTH, dk), lambda b, h: (0, blk + h))
    return pl.pallas_call(
        _dn_kernel,
        out_shape=jax.ShapeDtypeStruct((B * S, DN_W), BF16),
        grid=(B, DN_HEADS),
        in_specs=[
            pl.BlockSpec(memory_space=pltpu.SMEM),
            pl.BlockSpec(memory_space=pltpu.SMEM),
            seq(DN_Q_BLK), seq(DN_K_BLK), seq(DN_V_BLK), seq(DN_Z_BLK),
            pl.BlockSpec((S, LANES), lambda b, h: (b, DN_BA_BLK)),
            cw(0), cw(DN_HEADS), cw(2 * DN_HEADS),
            pl.BlockSpec((1, dk), lambda b, h: (0, 0)),
        ],
        out_specs=pl.BlockSpec((S, dk), lambda b, h: (b, h)),
        scratch_shapes=[pltpu.VMEM((S + CONV_PAD, dk), F32)] + [pltpu.VMEM((S, dk), F32)] * 7
        + [pltpu.VMEM((S, DN_CHUNK), F32)] + [pltpu.VMEM((S // DN_CHUNK * dk, dk), F32)] * 2
        + [pltpu.VMEM((S // DN_CHUNK * 8, dk), F32)],
        compiler_params=pltpu.CompilerParams(
            dimension_semantics=("arbitrary", "arbitrary"), vmem_limit_bytes=40 * MIB),
        name="deltanet",
    )(a_log, dt_bias, proj, proj, proj, proj, proj, conv_w, conv_w, conv_w,
      dn_norm_w.reshape(1, dk))


def _pack_bf16_pairs(x):
    c = x.shape[1] // 2
    bits = lambda t: lax.bitcast_convert_type(t.astype(BF16).astype(F32), jnp.uint32)
    return (bits(x[:, :c]) >> 16) | (bits(x[:, c:]) & jnp.uint32(0xFFFF0000))


def _unpack_bf16_pairs(p):
    lo = lax.bitcast_convert_type(p << 16, F32).astype(BF16)
    hi = lax.bitcast_convert_type(p & jnp.uint32(0xFFFF0000), F32).astype(BF16)
    return lo, hi


def _outproj_kernel(a_ref, d_ref, x_ref, wa_ref, wd_ref, nw_ref, wr_ref, br_ref,
                    h_ref, u_ref, ti_ref, tg_ref):
    h = x_ref[...] + _dot(a_ref[...], wa_ref[...]) + _dot(d_ref[...], wd_ref[...])
    h_ref[...] = h
    ms = jnp.mean(h * h, axis=-1, keepdims=True)
    u = h * lax.rsqrt(ms + NORM_EPS) * nw_ref[...]
    u_ref[...] = _pack_bf16_pairs(u)
    logits = _dot(u, wr_ref[...], HIGHEST) + br_ref[...]
    lane = lax.broadcasted_iota(jnp.int32, logits.shape, 1)
    lane_f = lane.astype(F32)
    vals = jnp.where(lane < N_EXPERTS, logits, -jnp.inf)
    top_v, top_i = [], []
    for _ in range(TOP_K):
        m = jnp.max(vals, axis=-1, keepdims=True)
        idx = jnp.min(jnp.where(vals == m, lane_f, float(LANES)), axis=-1, keepdims=True)
        top_v.append(m)
        top_i.append(idx)
        vals = jnp.where(lane_f == idx, -jnp.inf, vals)
    e = [jnp.exp(v - top_v[0]) for v in top_v]
    denom = e[0] + e[1] + e[2] + e[3]
    ti = jnp.zeros(logits.shape, F32)
    tg = jnp.zeros(logits.shape, F32)
    for k in range(TOP_K):
        ti = jnp.where(lane == k, top_i[k], ti)
        tg = jnp.where(lane == k, e[k] / denom, tg)
    ti_ref[...] = ti.astype(jnp.int32)
    tg_ref[...] = tg


def _outproj_router(attn_o, dn_o, x2d, w_out, ffn_norm_w, w_router, b_router, tm):
    T, D = x2d.shape
    wa = w_out[:ATTN_Q_W].astype(BF16)
    wd = w_out[ATTN_Q_W:].astype(BF16)
    wr = jnp.zeros((D, LANES), F32).at[:, :N_EXPERTS].set(w_router)
    br = jnp.zeros((1, LANES), F32).at[0, :N_EXPERTS].set(b_router)
    const = lambda shape: pl.BlockSpec(shape, lambda i: (0, 0))
    tile = lambda w: pl.BlockSpec((tm, w), lambda i: (i, 0))
    return pl.pallas_call(
        _outproj_kernel,
        out_shape=(jax.ShapeDtypeStruct((T, D), F32), jax.ShapeDtypeStruct((T, D // 2), jnp.uint32),
                   jax.ShapeDtypeStruct((T, LANES), jnp.int32),
                   jax.ShapeDtypeStruct((T, LANES), F32)),
        grid=(T // tm,),
        in_specs=[tile(ATTN_Q_W), tile(DN_W), tile(D), const((ATTN_Q_W, D)), const((DN_W, D)),
                  const((1, D)), const((D, LANES)), const((1, LANES))],
        out_specs=(tile(D), tile(D // 2), tile(LANES), tile(LANES)),
        compiler_params=pltpu.CompilerParams(
            dimension_semantics=("arbitrary",), vmem_limit_bytes=56 * MIB),
        name="outproj",
    )(attn_o, dn_o, x2d, wa, wd, ffn_norm_w.reshape(1, D), wr, br)


GATHER_UNROLL = 8


def _experts_kernel(ie_ref, ib_ref, in_ref, used_ref, tok_ref, u_ref, wg_ref, wl_ref, wd_ref,
                    bg_ref, bl_ref, bd_ref, y_ref, xraw, ybuf, wg_s, wl_s, wd_s, sem_x, sem_y):
    w = pl.program_id(0)
    j = pl.program_id(1)
    nj = pl.num_programs(1)
    R = MOE_ROWS
    nb = in_ref[w]
    row0 = ib_ref[w] * R

    def row_copy(i):
        return pltpu.make_async_copy(u_ref.at[pl.ds(tok_ref[row0 + i], 1), :],
                                     xraw.at[pl.ds(i, 1), :], sem_x)

    def y_copy(r):
        return pltpu.make_async_copy(ybuf.at[pl.ds(r * R, R), :],
                                     y_ref.at[pl.ds(row0 + r * R, R), :], sem_y)

    def for_blocks(fn):
        def body(r, _):
            fn(r)
            return 0
        lax.fori_loop(0, nb, body, 0)

    def rows_of(r):
        return pl.ds(pl.multiple_of(r * R, R), R)

    @pl.when(j == 0)
    def _():
        def issue(i, _):
            for t in range(GATHER_UNROLL):
                row_copy(i * GATHER_UNROLL + t).start()
            return 0

        lax.fori_loop(0, nb * (R // GATHER_UNROLL), issue, 0)
        bd = jnp.broadcast_to(bd_ref[...], (R, ybuf.shape[1]))

        def init(r):
            ybuf[rows_of(r), :] = bd

        for_blocks(init)

        for_blocks(lambda r: pltpu.make_async_copy(u_ref.at[pl.ds(0, R), :], xraw.at[rows_of(r), :],
                                                   sem_x).wait())

    @pl.when(nb > 0)
    def _():
        wg_s[...] = wg_ref[...].astype(BF16)
        wl_s[...] = wl_ref[...].astype(BF16)
        wd_s[...] = wd_ref[...].astype(BF16)

    bg = bg_ref[...]
    bl = bl_ref[...]

    def blocks(rs):
        rows = [rows_of(r) for r in rs]
        xs = [jnp.concatenate(_unpack_bf16_pairs(xraw[rw, :]), axis=1) for rw in rows]
        hg = [jnp.minimum(_dot(x, wg_s[...]) + bg, SWIGLU_LIMIT) for x in xs]
        hl = [jnp.clip(_dot(x, wl_s[...]) + bl, -SWIGLU_LIMIT, SWIGLU_LIMIT) for x in xs]
        acts = [(a * jax.nn.sigmoid(SWIGLU_ALPHA * a) * (b + 1.0)).astype(BF16) for a, b in zip(hg, hl)]
        for rw, act in zip(rows, acts):
            ybuf[rw, :] += _dot(act, wd_s[...])

    def pair(i, _):
        blocks([2 * i, 2 * i + 1])
        return 0

    lax.fori_loop(0, nb // 2, pair, 0)

    @pl.when(nb % 2 == 1)
    def _():
        blocks([nb - 1])

    @pl.when(j == nj - 1)
    def _():
        for_blocks(lambda r: y_copy(r).start())
        for_blocks(lambda r: y_copy(r).wait())

    @pl.when((j == nj - 1) & (w == pl.num_programs(0) - 1))
    def _():
        ybuf[0:R, :] = jnp.zeros((R, ybuf.shape[1]), F32)

        def tail_copy(r):
            return pltpu.make_async_copy(ybuf.at[pl.ds(0, R), :], y_ref.at[pl.ds(r * R, R), :], sem_y)

        def tail(fn):
            def body(r, _):
                fn(r)
                return 0
            lax.fori_loop(used_ref[0], y_ref.shape[0] // R, body, 0)

        tail(lambda r: tail_copy(r).start())
        tail(lambda r: tail_copy(r).wait())


def _experts(u_packed, row_tok, item_expert, item_blk0, item_nblk, n_blocks_used,
             w_gate_up, b_gate_up, w_down, b_down):
    P = row_tok.shape[0]
    D = u_packed.shape[1] * 2
    E, _, F2 = w_gate_up.shape
    F = F2 // 2
    tf = MOE_F_TILE
    nj = F // tf
    n_items = item_expert.shape[0]
    rows = MOE_ITEM_BLOCKS * MOE_ROWS
    b_gu = b_gate_up.reshape(E, 1, F2)
    b_d = b_down.reshape(E, 1, D)
    jt = lambda j, n: jnp.where(n > 0, j, nj - 1)
    return pl.pallas_call(
        _experts_kernel,
        out_shape=jax.ShapeDtypeStruct((P, D), F32),
        grid_spec=pltpu.PrefetchScalarGridSpec(
            num_scalar_prefetch=5,
            grid=(n_items, nj),
            in_specs=[
                pl.BlockSpec(memory_space=pl.ANY),
                pl.BlockSpec((None, D, tf), lambda w, j, ie, ib, nb, *_: (ie[w], 0, jt(j, nb[w]))),
                pl.BlockSpec((None, D, tf), lambda w, j, ie, ib, nb, *_: (ie[w], 0, nj + jt(j, nb[w]))),
                pl.BlockSpec((None, tf, D), lambda w, j, ie, ib, nb, *_: (ie[w], jt(j, nb[w]), 0)),
                pl.BlockSpec((None, 1, tf), lambda w, j, ie, ib, nb, *_: (ie[w], 0, jt(j, nb[w]))),
                pl.BlockSpec((None, 1, tf), lambda w, j, ie, ib, nb, *_: (ie[w], 0, nj + jt(j, nb[w]))),
                pl.BlockSpec((None, 1, D), lambda w, j, ie, *_: (ie[w], 0, 0)),
            ],
            out_specs=pl.BlockSpec(memory_space=pl.ANY),
            scratch_shapes=[
                pltpu.VMEM((rows, D // 2), jnp.uint32), pltpu.VMEM((rows, D), F32),
                pltpu.VMEM((D, tf), BF16), pltpu.VMEM((D, tf), BF16), pltpu.VMEM((tf, D), BF16),
                pltpu.SemaphoreType.DMA, pltpu.SemaphoreType.DMA,
            ],
        ),
        compiler_params=pltpu.CompilerParams(
            dimension_semantics=("arbitrary", "arbitrary"), vmem_limit_bytes=58 * MIB),
        name="experts",
    )(item_expert, item_blk0, item_nblk, n_blocks_used, row_tok, u_packed,
      w_gate_up, w_gate_up, w_down, b_gu, b_gu, b_d)


def _combine_kernel(dest_ref, h_ref, tg_ref, y_ref, o_ref, buf, sem):
    i = pl.program_id(0)
    n = pl.num_programs(0)
    tm = h_ref.shape[0]

    def issue_tile(step):
        slot = lax.rem(step, 2)

        def issue(r, _):
            for t in range(2):
                for k in range(TOP_K):
                    row = dest_ref[(step * tm + 2 * r + t) * TOP_K + k]
                    pltpu.make_async_copy(y_ref.at[pl.ds(row, 1), :],
                                          buf.at[slot, k, pl.ds(2 * r + t, 1), :], sem.at[slot]).start()
            return 0

        lax.fori_loop(0, tm // 2, issue, 0)

    @pl.when(i == 0)
    def _():
        issue_tile(0)

    @pl.when(i + 1 < n)
    def _():
        issue_tile(i + 1)

    slot = lax.rem(i, 2)
    for k in range(TOP_K):
        pltpu.make_async_copy(y_ref.at[pl.ds(0, tm), :], buf.at[slot, k], sem.at[slot]).wait()
    tg = tg_ref[...]
    acc = h_ref[...]
    for k in range(TOP_K):
        acc = acc + tg[:, k:k + 1] * buf[slot, k]
    o_ref[...] = acc


def _combine(h, top_g, y_sorted, dest, tm):
    T, D = h.shape
    return pl.pallas_call(
        _combine_kernel,
        out_shape=jax.ShapeDtypeStruct((T, D), F32),
        grid_spec=pltpu.PrefetchScalarGridSpec(
            num_scalar_prefetch=1,
            grid=(T // tm,),
            in_specs=[
                pl.BlockSpec((tm, D), lambda i, d: (i, 0)),
                pl.BlockSpec((tm, LANES), lambda i, d: (i, 0)),
                pl.BlockSpec(memory_space=pl.ANY),
            ],
            out_specs=pl.BlockSpec((tm, D), lambda i, d: (i, 0)),
            scratch_shapes=[pltpu.VMEM((2, TOP_K, tm, D), F32), pltpu.SemaphoreType.DMA((2,))],
        ),
        compiler_params=pltpu.CompilerParams(dimension_semantics=("arbitrary",)),
        name="combine",
    )(dest, h, top_g, y_sorted)


def _routing(top_i, T):
    R = MOE_ROWS
    n_assign = T * TOP_K
    e_flat = top_i[:, :TOP_K].reshape(n_assign)
    onehot = (e_flat[:, None] == jnp.arange(N_EXPERTS, dtype=jnp.int32)[None, :]).astype(jnp.int32)
    csum = jnp.cumsum(onehot, axis=0)
    rank = jnp.take_along_axis(csum, e_flat[:, None], axis=1)[:, 0] - 1
    counts = csum[-1]
    nblk = (counts + R - 1) // R
    blk_end = jnp.cumsum(nblk)
    blk_start = blk_end - nblk
    dest = blk_start[e_flat] * R + rank
    max_blocks = n_assign // R + N_EXPERTS
    P = max_blocks * R
    tok = jnp.arange(n_assign, dtype=jnp.int32) // TOP_K
    row_tok = jnp.zeros((P,), jnp.int32).at[dest].set(tok)
    nb_item = MOE_ITEM_BLOCKS
    n_items_e = (nblk + nb_item - 1) // nb_item
    item_end = jnp.cumsum(n_items_e)
    item_start = item_end - n_items_e
    max_items = N_EXPERTS + -(-max_blocks // nb_item)
    w = jnp.arange(max_items, dtype=jnp.int32)
    ie = jnp.minimum(jnp.searchsorted(item_end, w, side='right'), N_EXPERTS - 1).astype(jnp.int32)
    local = w - item_start[ie]
    live = w < item_end[-1]
    ib = blk_start[ie] + local * nb_item
    inb = jnp.clip(nblk[ie] - local * nb_item, 0, nb_item)
    last_e = ie[jnp.maximum(item_end[-1] - 1, 0)]
    ie = jnp.where(live, ie, last_e).astype(jnp.int32)
    ib = jnp.where(live, ib, 0).astype(jnp.int32)
    inb = jnp.where(live, inb, 0).astype(jnp.int32)
    return dest.astype(jnp.int32), row_tok, blk_end[-1:].astype(jnp.int32), P, ie, ib, inb


def _layer(x, attn_norm_w, w_in, q_norm_w, k_norm_w, attn_sinks, conv_w, a_log, dt_bias,
           dn_norm_w, w_out, ffn_norm_w, w_router, b_router, w_gate_up, b_gate_up, w_down, b_down):
    B, S, D = x.shape
    T = B * S
    x2d = x.reshape(T, D)
    proj = _inproj(x2d, attn_norm_w, w_in, tm=min(T, 1024), tn=512)
    attn_o = _attention(proj, attn_sinks, q_norm_w, k_norm_w, B, S)
    dn_o = _deltanet(proj, conv_w, a_log, dt_bias, dn_norm_w, B, S)
    h, u, top_i, top_g = _outproj_router(attn_o, dn_o, x2d, w_out, ffn_norm_w, w_router, b_router,
                                         tm=min(T, 512))
    dest, row_tok, n_blocks_used, P, ie, ib, inb = _routing(top_i, T)
    y_sorted = _experts(u, row_tok, ie, ib, inb, n_blocks_used, w_gate_up, b_gate_up, w_down, b_down)
    out = _combine(h, top_g, y_sorted, dest, tm=min(T, 128))
    return out.reshape(B, S, D)


def kernel(x, attn_norm_w, w_in, q_norm_w, k_norm_w, attn_sinks, conv_w, a_log, dt_bias, dn_norm_w,
           w_out, ffn_norm_w, w_router, b_router, w_gate_up, b_gate_up, w_down, b_down):
    h = x
    for l in range(attn_norm_w.shape[0]):
        h = _layer(h, attn_norm_w[l], w_in[l], q_norm_w[l], k_norm_w[l], attn_sinks[l], conv_w[l],
                   a_log[l], dt_bias[l], dn_norm_w[l], w_out[l], ffn_norm_w[l], w_router[l],
                   b_router[l], w_gate_up[l], b_gate_up[l], w_down[l], b_down[l])
    return h
```

```python
import functools
import math

import jax
import jax.numpy as jnp
from jax import lax
from jax.experimental import pallas as pl
from jax.experimental.pallas import tpu as pltpu

F32 = jnp.float32
BF16 = jnp.bfloat16

D_MODEL = 2048
ATTN_HEAD_DIM = 64
ATTN_HEADS = 16
ATTN_KV_HEADS = 4
ATTN_GROUP = ATTN_HEADS // ATTN_KV_HEADS
WINDOW = 128
DN_HEAD_DIM = 128
DN_HEADS = 8
DN_CHUNK = 64
CONV_WIDTH = 4
ATTN_Q_W = ATTN_HEADS * ATTN_HEAD_DIM
ATTN_KV_W = ATTN_KV_HEADS * ATTN_HEAD_DIM
DN_W = DN_HEADS * DN_HEAD_DIM
IN_WIDTH = ATTN_Q_W + 2 * ATTN_KV_W + 4 * DN_W + 2 * DN_HEADS
N_EXPERTS = 32
TOP_K = 4
D_EXPERT = 2048
SWIGLU_ALPHA = 1.702
SWIGLU_LIMIT = 7.0
NORM_EPS = 1e-5
QK_NORM_EPS = 1e-6
DN_NORM_EPS = 1e-6
L2_EPS = 1e-6

LANES = 128
MIB = 1024 * 1024
HIGHEST = lax.Precision.HIGHEST

ATTN_K_BLK = ATTN_Q_W // ATTN_KV_W
ATTN_V_BLK = ATTN_K_BLK + 1
DN_Q_BLK = (ATTN_Q_W + 2 * ATTN_KV_W) // DN_HEAD_DIM
DN_K_BLK = DN_Q_BLK + DN_HEADS
DN_V_BLK = DN_K_BLK + DN_HEADS
DN_Z_BLK = DN_V_BLK + DN_HEADS
DN_BA_BLK = DN_Z_BLK + DN_HEADS

MOE_ROWS = 256
MOE_ITEM_BLOCKS = 9
MOE_F_TILE = 256


def _dot(a, b, precision=None):
    return jnp.dot(a, b, preferred_element_type=F32, precision=precision)


def _dot_nt(a, b, precision=None):
    return lax.dot_general(a, b, (((1,), (1,)), ((), ())), preferred_element_type=F32,
                           precision=precision)


def _dot_tn(a, b, precision=None):
    return lax.dot_general(a, b, (((0,), (0,)), ((), ())), preferred_element_type=F32,
                           precision=precision)


def _inproj_kernel(x_ref, nw_ref, w_ref, o_ref, u_ref, *, last_width):
    @pl.when(pl.program_id(1) == 0)
    def _():
        x = x_ref[...]
        ms = jnp.mean(x * x, axis=-1, keepdims=True)
        u_ref[...] = (x * lax.rsqrt(ms + NORM_EPS) * nw_ref[...]).astype(BF16)

    last = pl.num_programs(1) - 1

    @pl.when(pl.program_id(1) < last)
    def _():
        o_ref[...] = _dot(u_ref[...], w_ref[...].astype(BF16))

    @pl.when(pl.program_id(1) == last)
    def _():
        o_ref[:, :last_width] = _dot(u_ref[...], w_ref[:, :last_width].astype(BF16))


def _inproj(x2d, norm_w, w_in, tm, tn):
    T, D = x2d.shape
    N = w_in.shape[1]
    n_col_blocks = pl.cdiv(N, tn)
    last_width = min(tn, -(-(N - (n_col_blocks - 1) * tn) // LANES) * LANES)
    return pl.pallas_call(
        functools.partial(_inproj_kernel, last_width=last_width),
        out_shape=jax.ShapeDtypeStruct((T, N), F32),
        grid=(T // tm, pl.cdiv(N, tn)),
        in_specs=[
            pl.BlockSpec((tm, D), lambda i, j: (i, 0)),
            pl.BlockSpec((1, D), lambda i, j: (0, 0)),
            pl.BlockSpec((D, tn), lambda i, j: (0, j)),
        ],
        out_specs=pl.BlockSpec((tm, tn), lambda i, j: (i, j)),
        scratch_shapes=[pltpu.VMEM((tm, D), BF16)],
        compiler_params=pltpu.CompilerParams(
            dimension_semantics=("arbitrary", "arbitrary"), vmem_limit_bytes=56 * MIB),
        name="inproj",
    )(x2d, norm_w.reshape(1, D), w_in)


def _attn_kernel(sink_ref, q_ref, kp_ref, kc_ref, vp_ref, vc_ref, qw_ref, kw_ref, o_ref):
    n = pl.program_id(1)
    L = WINDOW
    d = ATTN_HEAD_DIM

    def head_norm(t, w):
        ms = jnp.mean(t * t, axis=-1, keepdims=True)
        return t * lax.rsqrt(ms + QK_NORM_EPS) * w

    G = ATTN_GROUP
    qi = lax.broadcasted_iota(jnp.int32, (G * L, 2 * L), 0)
    kj = lax.broadcasted_iota(jnp.int32, (G * L, 2 * L), 1)
    dist = (qi & (L - 1)) + L - kj
    valid = (dist >= 0) & (dist < WINDOW) & ((kj >= L) | (n > 0))
    dist_f = dist.astype(F32)
    grp = lax.broadcasted_iota(jnp.int32, (G * L, 1), 0) >> (L.bit_length() - 1)

    def per_row(vals):
        col = jnp.full((G * L, 1), vals[G - 1], F32)
        for g in range(G - 2, -1, -1):
            col = jnp.where(grp == g, vals[g], col)
        return col

    qw = qw_ref[...]
    kw = kw_ref[...]
    kvs = range(ATTN_KV_HEADS)
    heads = lambda kv: [kv * G + g for g in range(G)]
    cols = lambda h: slice(h * d, (h + 1) * d)
    k = [head_norm(jnp.concatenate([kp_ref[:, cols(kv)], kc_ref[:, cols(kv)]], axis=0), kw).astype(BF16)
         for kv in kvs]
    v = [jnp.concatenate([vp_ref[:, cols(kv)], vc_ref[:, cols(kv)]], axis=0).astype(BF16) for kv in kvs]
    q = [head_norm(jnp.concatenate([q_ref[:, cols(h)] for h in heads(kv)], axis=0), qw).astype(BF16)
         for kv in kvs]
    s = [_dot_nt(q[kv], k[kv]) * (d ** -0.5) for kv in kvs]
    slope = [per_row([2.0 ** (-8.0 * (h + 1) / ATTN_HEADS) for h in heads(kv)]) for kv in kvs]
    sink = [per_row([sink_ref[h] for h in heads(kv)]) for kv in kvs]
    s = [jnp.where(valid, s[kv] - slope[kv] * dist_f, -jnp.inf) for kv in kvs]
    m = [jnp.maximum(jnp.max(s[kv], axis=-1, keepdims=True), sink[kv]) for kv in kvs]
    p = [jnp.exp(s[kv] - m[kv]) for kv in kvs]
    denom = [jnp.sum(p[kv], axis=-1, keepdims=True) + jnp.exp(sink[kv] - m[kv]) for kv in kvs]
    probs = [(p[kv] / denom[kv]).astype(BF16) for kv in kvs]
    o = [_dot(probs[kv], v[kv]) for kv in kvs]
    for kv in kvs:
        for g, h in enumerate(heads(kv)):
            o_ref[:, cols(h)] = o[kv][g * L:(g + 1) * L, :].astype(o_ref.dtype)


def _attention(proj, sinks, q_norm_w, k_norm_w, B, S):
    nb = S // WINDOW
    L = WINDOW
    row = lambda b, n: b * nb + n
    prev = lambda b, n: b * nb + jnp.maximum(n - 1, 0)
    return pl.pallas_call(
        _attn_kernel,
        out_shape=jax.ShapeDtypeStruct((B * S, ATTN_Q_W), BF16),
        grid=(B, nb),
        in_specs=[
            pl.BlockSpec(memory_space=pltpu.SMEM),
            pl.BlockSpec((L, ATTN_Q_W), lambda b, n: (row(b, n), 0)),
            pl.BlockSpec((L, ATTN_KV_W), lambda b, n: (prev(b, n), ATTN_K_BLK)),
            pl.BlockSpec((L, ATTN_KV_W), lambda b, n: (row(b, n), ATTN_K_BLK)),
            pl.BlockSpec((L, ATTN_KV_W), lambda b, n: (prev(b, n), ATTN_V_BLK)),
            pl.BlockSpec((L, ATTN_KV_W), lambda b, n: (row(b, n), ATTN_V_BLK)),
            pl.BlockSpec((1, ATTN_HEAD_DIM), lambda b, n: (0, 0)),
            pl.BlockSpec((1, ATTN_HEAD_DIM), lambda b, n: (0, 0)),
        ],
        out_specs=pl.BlockSpec((L, ATTN_Q_W), lambda b, n: (row(b, n), 0)),
        compiler_params=pltpu.CompilerParams(dimension_semantics=("arbitrary", "arbitrary")),
        name="attn",
    )(sinks, proj, proj, proj, proj, proj, q_norm_w.reshape(1, -1), k_norm_w.reshape(1, -1))


CONV_PAD = 8


DN_GROUP = 16

_X3 = ((0, 0), (1, 0), (0, 1))
_EXACT_LHS = ((0, 0), (0, 1), (0, 2))
_EXACT_RHS = ((0, 0), (1, 0), (2, 0))


def _split(x, n):
    parts = []
    for i in range(n):
        p = x.astype(BF16)
        parts.append(p)
        if i + 1 < n:
            x = x - p.astype(F32)
    return parts


def _mm(a_parts, b_parts, terms, form='nn'):
    a_axis = 0 if form == 'tn' else 1
    b_axis = 1 if form == 'nt' else 0
    a = jnp.concatenate([a_parts[i] for i, _ in terms], axis=a_axis)
    b = jnp.concatenate([b_parts[j] for _, j in terms], axis=b_axis)
    return {'nn': _dot, 'nt': _dot_nt, 'tn': _dot_tn}[form](a, b)


def _dn_kernel(alog_ref, dtb_ref, q_ref, k_ref, v_ref, z_ref, ba_ref, cq_ref, ck_ref, cv_ref,
               nw_ref, o_ref, xp, qs, ks, vs, gs, bs, us, ws, am, pm, qm, gl):
    h = pl.program_id(1)
    S = q_ref.shape[0]
    C = DN_CHUNK
    dk = DN_HEAD_DIM
    piece = min(S, 256)

    def conv_silu(x_ref, cw_ref, dst, l2):
        xp[0:CONV_PAD, :] = jnp.zeros((CONV_PAD, dk), F32)
        xp[CONV_PAD:CONV_PAD + S, :] = x_ref[...]
        for p in range(S // piece):
            r0 = p * piece
            acc = jnp.zeros((piece, dk), F32)
            for i in range(CONV_WIDTH):
                off = CONV_PAD + r0 - (CONV_WIDTH - 1) + i
                acc = acc + xp[off:off + piece, :] * cw_ref[i:i + 1, :]
            y = acc * jax.nn.sigmoid(acc)
            if l2:
                y = y * lax.rsqrt(jnp.sum(y * y, axis=-1, keepdims=True) + L2_EPS)
            dst[r0:r0 + piece, :] = y

    conv_silu(q_ref, cq_ref, qs, True)
    conv_silu(k_ref, ck_ref, ks, True)
    conv_silu(v_ref, cv_ref, vs, False)

    neg_a = -jnp.exp(jnp.full((1, dk), alog_ref[h], F32))
    dtb = dtb_ref[h]
    for p in range(S // piece):
        r0 = p * piece
        ba = ba_ref[r0:r0 + piece, :]
        lane = lax.broadcasted_iota(jnp.int32, ba.shape, 1)
        b_col = jnp.sum(jnp.where(lane == h, ba, 0.0), axis=-1, keepdims=True)
        a_col = jnp.sum(jnp.where(lane == DN_HEADS + h, ba, 0.0), axis=-1, keepdims=True)
        bs[r0:r0 + piece, :] = jnp.broadcast_to(jax.nn.sigmoid(b_col), (piece, dk))
        gs[r0:r0 + piece, :] = neg_a * jnp.broadcast_to(jax.nn.softplus(a_col + dtb), (piece, dk))

    ri = lax.broadcasted_iota(jnp.int32, (C, C), 0)
    ci = lax.broadcasted_iota(jnp.int32, (C, C), 1)
    incl = ri >= ci
    strict = ri > ci
    tri_ones = [jnp.where(incl, 1.0, 0.0).astype(BF16)]
    upper_ones = [jnp.where(ri <= ci, 1.0, 0.0).astype(BF16)]
    nw = nw_ref[...]
    n_chunks = S // C
    group = DN_GROUP if n_chunks % DN_GROUP == 0 else 1

    def intra_load(c):
        rows = pl.ds(pl.multiple_of(c * C, C), C)
        return qs[rows, :], ks[rows, :], vs[rows, :], gs[rows, :], bs[rows, :]

    def intra_compute(loaded):
        each = lambda f, *ls: [f(*a) for a in zip(*ls)]
        q, k, v, g, beta = (list(t) for t in zip(*loaded))
        q = each(lambda t: t * (dk ** -0.5), q)
        g3 = each(lambda t: _split(t, 3), g)
        gc = each(lambda p: _mm(tri_ones, p, _EXACT_LHS), g3)
        gc_row = each(lambda p: _mm([t[:, :C] for t in p], upper_ones, _EXACT_RHS, 'tn'), g3)
        decay = each(lambda a, b: jnp.exp(jnp.where(incl, a[:, :C] - b, -jnp.inf)), gc, gc_row)
        kb = each(jnp.multiply, k, beta)
        k2 = each(lambda t: _split(t, 2), k)
        kk = each(lambda a, b: _mm(_split(a, 2), b, _X3, 'nt'), kb, k2)
        qk = each(lambda a, b: _mm(_split(a, 2), b, _X3, 'nt'), q, k2)
        m = each(lambda a, d: -jnp.where(strict, a * d, 0.0), kk, decay)
        x = each(lambda vv, bb, kbb, gg: jnp.concatenate([vv * bb, kbb * jnp.exp(gg)], axis=1),
                 v, beta, kb, gc)
        n_fac = int(math.log2(C))
        for it in range(n_fac):
            m2 = each(lambda t: _split(t, 2), m)
            x = each(lambda xx, mm: xx + _mm(mm, _split(xx, 2), _X3), x, m2)
            if it + 1 < n_fac:
                m = each(lambda mm: _mm(mm, mm, _X3), m2)
        kd2 = each(lambda kk_, gg: _split(kk_ * jnp.exp(gg[C - 1:C, :] - gg), 2), k, gc)
        pq = each(lambda a, xx: _mm(a, _split(xx, 2), _X3, 'tn'), kd2, x)
        a_intra = each(lambda a, d: jnp.where(incl, a * d, 0.0), qk, decay)
        qg = each(lambda a, gg: a * jnp.exp(gg), q, gc)
        e_last = each(lambda gg: jnp.broadcast_to(jnp.exp(gg[C - 1:C, :]), (8, dk)), gc)
        return list(zip(x, qg, a_intra, pq, e_last))

    def intra_store(c, x, qg, a_intra, pq, e_last):
        rows = pl.ds(pl.multiple_of(c * C, C), C)
        prow = pl.ds(pl.multiple_of(c * dk, dk), dk)
        us[rows, :] = x[:, :dk]
        ws[rows, :] = x[:, dk:]
        qs[rows, :] = qg
        am[rows, :] = a_intra
        qm[prow, :] = pq[:, :dk]
        pm[prow, :] = pq[:, dk:]
        gl[pl.ds(pl.multiple_of(c * 8, 8), 8), :] = e_last

    def intra_group(i, _):
        loaded = [intra_load(i * group + t) for t in range(group)]
        outs = intra_compute(loaded)
        for t in range(group):
            intra_store(i * group + t, *outs[t])
        return 0

    lax.fori_loop(0, n_chunks // group, intra_group, 0)

    def scan(c, state):
        prow = pl.ds(pl.multiple_of(c * dk, dk), dk)
        ps = _mm(_split(pm[prow, :], 2), _split(state, 2), _X3)
        pm[prow, :] = state
        e_last = gl[pl.ds(pl.multiple_of(c * 8, 8), 1), :]
        return state * e_last - ps + qm[prow, :]

    lax.fori_loop(0, n_chunks, scan, jnp.zeros((dk, dk), F32), unroll=2)

    def out_group(i, _):
        cs = [i * group + t for t in range(group)]
        rows = [pl.ds(pl.multiple_of(c * C, C), C) for c in cs]
        each = lambda f, *ls: [f(*a) for a in zip(*ls)]
        state2 = each(lambda c: _split(pm[pl.ds(pl.multiple_of(c * dk, dk), dk), :], 2), cs)
        lhs2 = each(lambda r: _split(jnp.concatenate([ws[r, :], qs[r, :]], axis=0), 2), rows)
        prod = each(lambda a, b: _mm(a, b, _X3), lhs2, state2)
        v_new = each(lambda r, p: us[r, :] - p[:C], rows, prod)
        av = each(lambda r, vn: _mm(_split(am[r, :], 2), _split(vn, 2), _X3), rows, v_new)
        o = each(lambda p, a: p[C:] + a, prod, av)
        o = each(lambda t: t * lax.rsqrt(jnp.mean(t * t, axis=-1, keepdims=True) + DN_NORM_EPS) * nw, o)
        for r, t in zip(rows, o):
            z = z_ref[r, :]
            o_ref[r, :] = (t * (z * jax.nn.sigmoid(z))).astype(o_ref.dtype)
        return 0

    lax.fori_loop(0, n_chunks // group, out_group, 0)


def _deltanet(proj, conv_w, a_log, dt_bias, dn_norm_w, B, S):
    dk = DN_HEAD_DIM
    seq = lambda blk: pl.BlockSpec((S, dk), lambda b, h: (b, blk + h))
    cw = lambda blk: pl.BlockSpec((CONV_WIDTH, dk), lambda b, h: (0, blk + h))
    return pl.pallas_call(
        _dn_kernel,
        out_shape=jax.ShapeDtypeStruct((B * S, DN_W), BF16),
        grid=(B, DN_HEADS),
        in_specs=[
            pl.BlockSpec(memory_space=pltpu.SMEM),
            pl.BlockSpec(memory_space=pltpu.SMEM),
            seq(DN_Q_BLK), seq(DN_K_BLK), seq(DN_V_BLK), seq(DN_Z_BLK),
            pl.BlockSpec((S, LANES), lambda b, h: (b, DN_BA_BLK)),
            cw(0), cw(DN_HEADS), cw(2 * DN_HEADS),
            pl.BlockSpec((1, dk), lambda b, h: (0, 0)),
        ],
        out_specs=pl.BlockSpec((S, dk), lambda b, h: (b, h)),
        scratch_shapes=[pltpu.VMEM((S + CONV_PAD, dk), F32)] + [pltpu.VMEM((S, dk), F32)] * 7
        + [pltpu.VMEM((S, DN_CHUNK), F32)] + [pltpu.VMEM((S // DN_CHUNK * dk, dk), F32)] * 2
        + [pltpu.VMEM((S // DN_CHUNK * 8, dk), F32)],
        compiler_params=pltpu.CompilerParams(
            dimension_semantics=("arbitrary", "arbitrary"), vmem_limit_bytes=40 * MIB),
        name="deltanet",
    )(a_log, dt_bias, proj, proj, proj, proj, proj, conv_w, conv_w, conv_w,
      dn_norm_w.reshape(1, dk))


def _pack_bf16_pairs(x):
    c = x.shape[1] // 2
    bits = lambda t: lax.bitcast_convert_type(t.astype(BF16).astype(F32), jnp.uint32)
    return (bits(x[:, :c]) >> 16) | (bits(x[:, c:]) & jnp.uint32(0xFFFF0000))


def _unpack_bf16_pairs(p):
    lo = lax.bitcast_convert_type(p << 16, F32).astype(BF16)
    hi = lax.bitcast_convert_type(p & jnp.uint32(0xFFFF0000), F32).astype(BF16)
    return lo, hi


def _outproj_kernel(a_ref, d_ref, x_ref, wa_ref, wd_ref, nw_ref, wrh_ref, wrl_ref, br_ref,
                    h_ref, u_ref, ti_ref, tg_ref):
    h = x_ref[...] + _dot(a_ref[...], wa_ref[...]) + _dot(d_ref[...], wd_ref[...])
    h_ref[...] = h
    ms = jnp.mean(h * h, axis=-1, keepdims=True)
    u = h * lax.rsqrt(ms + NORM_EPS) * nw_ref[...]
    u_ref[...] = _pack_bf16_pairs(u)
    u_hi, u_lo = _split(u, 2)
    logits = (_dot(u_hi, wrh_ref[...]) + _dot(u_lo, wrh_ref[...]) + _dot(u_hi, wrl_ref[...])
              + br_ref[...])
    lane = lax.broadcasted_iota(jnp.int32, logits.shape, 1)
    lane_f = lane.astype(F32)
    vals = jnp.where(lane < N_EXPERTS, logits, -jnp.inf)
    top_v, top_i = [], []
    for _ in range(TOP_K):
        m = jnp.max(vals, axis=-1, keepdims=True)
        idx = jnp.min(jnp.where(vals == m, lane_f, float(LANES)), axis=-1, keepdims=True)
        top_v.append(m)
        top_i.append(idx)
        vals = jnp.where(lane_f == idx, -jnp.inf, vals)
    e = [jnp.exp(v - top_v[0]) for v in top_v]
    denom = e[0] + e[1] + e[2] + e[3]
    ti = jnp.zeros(logits.shape, F32)
    tg = jnp.zeros(logits.shape, F32)
    for k in range(TOP_K):
        ti = jnp.where(lane == k, top_i[k], ti)
        tg = jnp.where(lane == k, e[k] / denom, tg)
    ti_ref[...] = ti.astype(jnp.int32)
    tg_ref[...] = tg


def _outproj_router(attn_o, dn_o, x2d, w_out, ffn_norm_w, w_router, b_router, tm):
    T, D = x2d.shape
    wa = w_out[:ATTN_Q_W].astype(BF16)
    wd = w_out[ATTN_Q_W:].astype(BF16)
    wr = jnp.zeros((D, LANES), F32).at[:, :N_EXPERTS].set(w_router)
    wr_hi = wr.astype(BF16)
    wr_lo = (wr - wr_hi.astype(F32)).astype(BF16)
    br = jnp.zeros((1, LANES), F32).at[0, :N_EXPERTS].set(b_router)
    const = lambda shape: pl.BlockSpec(shape, lambda i: (0, 0))
    tile = lambda w: pl.BlockSpec((tm, w), lambda i: (i, 0))
    return pl.pallas_call(
        _outproj_kernel,
        out_shape=(jax.ShapeDtypeStruct((T, D), F32), jax.ShapeDtypeStruct((T, D // 2), jnp.uint32),
                   jax.ShapeDtypeStruct((T, LANES), jnp.int32),
                   jax.ShapeDtypeStruct((T, LANES), F32)),
        grid=(T // tm,),
        in_specs=[tile(ATTN_Q_W), tile(DN_W), tile(D), const((ATTN_Q_W, D)), const((DN_W, D)),
                  const((1, D)), const((D, LANES)), const((D, LANES)), const((1, LANES))],
        out_specs=(tile(D), tile(D // 2), tile(LANES), tile(LANES)),
        compiler_params=pltpu.CompilerParams(
            dimension_semantics=("arbitrary",), vmem_limit_bytes=56 * MIB),
        name="outproj",
    )(attn_o, dn_o, x2d, wa, wd, ffn_norm_w.reshape(1, D), wr_hi, wr_lo, br)


GATHER_UNROLL = 8


def _experts_kernel(ie_ref, ib_ref, in_ref, used_ref, tok_ref, u_ref, wg_ref, wl_ref, wd_ref,
                    bg_ref, bl_ref, bd_ref, y_ref, xraw, ybuf, wg_s, wl_s, wd_s, sem_x, sem_y):
    w = pl.program_id(0)
    j = pl.program_id(1)
    nj = pl.num_programs(1)
    R = MOE_ROWS
    nb = in_ref[w]
    row0 = ib_ref[w] * R

    def row_copy(i):
        return pltpu.make_async_copy(u_ref.at[pl.ds(tok_ref[row0 + i], 1), :],
                                     xraw.at[pl.ds(i, 1), :], sem_x)

    def y_copy(r):
        return pltpu.make_async_copy(ybuf.at[pl.ds(r * R, R), :],
                                     y_ref.at[pl.ds(row0 + r * R, R), :], sem_y)

    def for_blocks(fn):
        def body(r, _):
            fn(r)
            return 0
        lax.fori_loop(0, nb, body, 0)

    def rows_of(r):
        return pl.ds(pl.multiple_of(r * R, R), R)

    @pl.when(j == 0)
    def _():
        def issue(i, _):
            for t in range(GATHER_UNROLL):
                row_copy(i * GATHER_UNROLL + t).start()
            return 0

        lax.fori_loop(0, nb * (R // GATHER_UNROLL), issue, 0)
        bd = jnp.broadcast_to(bd_ref[...], (R, ybuf.shape[1]))

        def init(r):
            ybuf[rows_of(r), :] = bd

        for_blocks(init)

        for_blocks(lambda r: pltpu.make_async_copy(u_ref.at[pl.ds(0, R), :], xraw.at[rows_of(r), :],
                                                   sem_x).wait())

    @pl.when(nb > 0)
    def _():
        wg_s[...] = wg_ref[...].astype(BF16)
        wl_s[...] = wl_ref[...].astype(BF16)
        wd_s[...] = wd_ref[...].astype(BF16)

    bg = bg_ref[...]
    bl = bl_ref[...]

    def blocks(rs):
        rows = [rows_of(r) for r in rs]
        xs = [jnp.concatenate(_unpack_bf16_pairs(xraw[rw, :]), axis=1) for rw in rows]
        hg = [jnp.minimum(_dot(x, wg_s[...]) + bg, SWIGLU_LIMIT) for x in xs]
        hl = [jnp.clip(_dot(x, wl_s[...]) + bl, -SWIGLU_LIMIT, SWIGLU_LIMIT) for x in xs]
        acts = [(a * jax.nn.sigmoid(SWIGLU_ALPHA * a) * (b + 1.0)).astype(BF16) for a, b in zip(hg, hl)]
        for rw, act in zip(rows, acts):
            ybuf[rw, :] += _dot(act, wd_s[...])

    def pair(i, _):
        blocks([2 * i, 2 * i + 1])
        return 0

    lax.fori_loop(0, nb // 2, pair, 0)

    @pl.when(nb % 2 == 1)
    def _():
        blocks([nb - 1])

    @pl.when(j == nj - 1)
    def _():
        for_blocks(lambda r: y_copy(r).start())
        for_blocks(lambda r: y_copy(r).wait())

    @pl.when((j == nj - 1) & (w == pl.num_programs(0) - 1))
    def _():
        ybuf[0:R, :] = jnp.zeros((R, ybuf.shape[1]), F32)

        def tail_copy(r):
            return pltpu.make_async_copy(ybuf.at[pl.ds(0, R), :], y_ref.at[pl.ds(r * R, R), :], sem_y)

        def tail(fn):
            def body(r, _):
                fn(r)
                return 0
            lax.fori_loop(used_ref[0], y_ref.shape[0] // R, body, 0)

        tail(lambda r: tail_copy(r).start())
        tail(lambda r: tail_copy(r).wait())


def _experts(u_packed, row_tok, item_expert, item_blk0, item_nblk, n_blocks_used,
             w_gate_up, b_gate_up, w_down, b_down):
    P = row_tok.shape[0]
    D = u_packed.shape[1] * 2
    E, _, F2 = w_gate_up.shape
    F = F2 // 2
    tf = MOE_F_TILE
    nj = F // tf
    n_items = item_expert.shape[0]
    rows = MOE_ITEM_BLOCKS * MOE_ROWS
    b_gu = b_gate_up.reshape(E, 1, F2)
    b_d = b_down.reshape(E, 1, D)
    jt = lambda j, n: jnp.where(n > 0, j, nj - 1)
    return pl.pallas_call(
        _experts_kernel,
        out_shape=jax.ShapeDtypeStruct((P, D), F32),
        grid_spec=pltpu.PrefetchScalarGridSpec(
            num_scalar_prefetch=5,
            grid=(n_items, nj),
            in_specs=[
                pl.BlockSpec(memory_space=pl.ANY),
                pl.BlockSpec((None, D, tf), lambda w, j, ie, ib, nb, *_: (ie[w], 0, jt(j, nb[w]))),
                pl.BlockSpec((None, D, tf), lambda w, j, ie, ib, nb, *_: (ie[w], 0, nj + jt(j, nb[w]))),
                pl.BlockSpec((None, tf, D), lambda w, j, ie, ib, nb, *_: (ie[w], jt(j, nb[w]), 0)),
                pl.BlockSpec((None, 1, tf), lambda w, j, ie, ib, nb, *_: (ie[w], 0, jt(j, nb[w]))),
                pl.BlockSpec((None, 1, tf), lambda w, j, ie, ib, nb, *_: (ie[w], 0, nj + jt(j, nb[w]))),
                pl.BlockSpec((None, 1, D), lambda w, j, ie, *_: (ie[w], 0, 0)),
            ],
            out_specs=pl.BlockSpec(memory_space=pl.ANY),
            scratch_shapes=[
                pltpu.VMEM((rows, D // 2), jnp.uint32), pltpu.VMEM((rows, D), F32),
                pltpu.VMEM((D, tf), BF16), pltpu.VMEM((D, tf), BF16), pltpu.VMEM((tf, D), BF16),
                pltpu.SemaphoreType.DMA, pltpu.SemaphoreType.DMA,
            ],
        ),
        compiler_params=pltpu.CompilerParams(
            dimension_semantics=("arbitrary", "arbitrary"), vmem_limit_bytes=58 * MIB),
        name="experts",
    )(item_expert, item_blk0, item_nblk, n_blocks_used, row_tok, u_packed,
      w_gate_up, w_gate_up, w_down, b_gu, b_gu, b_d)


def _combine_kernel(dest_ref, h_ref, tg_ref, y_ref, o_ref, buf, sem):
    i = pl.program_id(0)
    n = pl.num_programs(0)
    tm = h_ref.shape[0]

    def issue_tile(step):
        slot = lax.rem(step, 2)

        def issue(r, _):
            for t in range(2):
                for k in range(TOP_K):
                    row = dest_ref[(step * tm + 2 * r + t) * TOP_K + k]
                    pltpu.make_async_copy(y_ref.at[pl.ds(row, 1), :],
                                          buf.at[slot, k, pl.ds(2 * r + t, 1), :], sem.at[slot]).start()
            return 0

        lax.fori_loop(0, tm // 2, issue, 0)

    @pl.when(i == 0)
    def _():
        issue_tile(0)

    @pl.when(i + 1 < n)
    def _():
        issue_tile(i + 1)

    slot = lax.rem(i, 2)
    for k in range(TOP_K):
        pltpu.make_async_copy(y_ref.at[pl.ds(0, tm), :], buf.at[slot, k], sem.at[slot]).wait()
    tg = tg_ref[...]
    acc = h_ref[...]
    for k in range(TOP_K):
        acc = acc + tg[:, k:k + 1] * buf[slot, k]
    o_ref[...] = acc


def _combine(h, top_g, y_sorted, dest, tm):
    T, D = h.shape
    return pl.pallas_call(
        _combine_kernel,
        out_shape=jax.ShapeDtypeStruct((T, D), F32),
        grid_spec=pltpu.PrefetchScalarGridSpec(
            num_scalar_prefetch=1,
            grid=(T // tm,),
            in_specs=[
                pl.BlockSpec((tm, D), lambda i, d: (i, 0)),
                pl.BlockSpec((tm, LANES), lambda i, d: (i, 0)),
                pl.BlockSpec(memory_space=pl.ANY),
            ],
            out_specs=pl.BlockSpec((tm, D), lambda i, d: (i, 0)),
            scratch_shapes=[pltpu.VMEM((2, TOP_K, tm, D), F32), pltpu.SemaphoreType.DMA((2,))],
        ),
        compiler_params=pltpu.CompilerParams(dimension_semantics=("arbitrary",)),
        name="combine",
    )(dest, h, top_g, y_sorted)


INVERT_UNROLL = 8


def _invert_kernel(dest_ref, row_end_ref, blk_end_ref, tok_ref):
    def zero(i, _):
        tok_ref[i] = 0
        return 0

    def pad(e, _):
        hi = jnp.where(e + 1 < N_EXPERTS, blk_end_ref[e] * MOE_ROWS, tok_ref.shape[0])
        lax.fori_loop(row_end_ref[e], hi, zero, 0)
        return 0

    lax.fori_loop(0, N_EXPERTS, pad, 0)

    def put(i, _):
        for t in range(INVERT_UNROLL):
            a = i * INVERT_UNROLL + t
            tok_ref[dest_ref[a]] = a // TOP_K
        return 0

    lax.fori_loop(0, dest_ref.shape[0] // INVERT_UNROLL, put, 0)


def _invert(dest, row_end, blk_end, P):
    return pl.pallas_call(
        _invert_kernel,
        out_shape=jax.ShapeDtypeStruct((P,), jnp.int32),
        in_specs=[pl.BlockSpec(memory_space=pltpu.SMEM)] * 3,
        out_specs=pl.BlockSpec(memory_space=pltpu.SMEM),
        name="invert",
    )(dest, row_end, blk_end)


def _routing(top_i, T):
    R = MOE_ROWS
    n_assign = T * TOP_K
    e_flat = top_i[:, :TOP_K].reshape(n_assign)
    onehot = (e_flat[:, None] == jnp.arange(N_EXPERTS, dtype=jnp.int32)[None, :]).astype(jnp.int32)
    csum = jnp.cumsum(onehot, axis=0)
    rank = jnp.take_along_axis(csum, e_flat[:, None], axis=1)[:, 0] - 1
    counts = csum[-1]
    nblk = (counts + R - 1) // R
    blk_end = jnp.cumsum(nblk)
    blk_start = blk_end - nblk
    dest = blk_start[e_flat] * R + rank
    max_blocks = n_assign // R + N_EXPERTS
    P = max_blocks * R
    dest = dest.astype(jnp.int32)
    row_end = (blk_start * R + counts).astype(jnp.int32)
    row_tok = _invert(dest, row_end, blk_end.astype(jnp.int32), P)
    nb_item = MOE_ITEM_BLOCKS
    n_items_e = (nblk + nb_item - 1) // nb_item
    item_end = jnp.cumsum(n_items_e)
    item_start = item_end - n_items_e
    max_items = N_EXPERTS + -(-max_blocks // nb_item)
    w = jnp.arange(max_items, dtype=jnp.int32)
    ie = jnp.minimum(jnp.searchsorted(item_end, w, side='right'), N_EXPERTS - 1).astype(jnp.int32)
    local = w - item_start[ie]
    live = w < item_end[-1]
    ib = blk_start[ie] + local * nb_item
    inb = jnp.clip(nblk[ie] - local * nb_item, 0, nb_item)
    last_e = ie[jnp.maximum(item_end[-1] - 1, 0)]
    ie = jnp.where(live, ie, last_e).astype(jnp.int32)
    ib = jnp.where(live, ib, 0).astype(jnp.int32)
    inb = jnp.where(live, inb, 0).astype(jnp.int32)
    return dest.astype(jnp.int32), row_tok, blk_end[-1:].astype(jnp.int32), P, ie, ib, inb


def _layer(x, attn_norm_w, w_in, q_norm_w, k_norm_w, attn_sinks, conv_w, a_log, dt_bias,
           dn_norm_w, w_out, ffn_norm_w, w_router, b_router, w_gate_up, b_gate_up, w_down, b_down):
    B, S, D = x.shape
    T = B * S
    x2d = x.reshape(T, D)
    proj = _inproj(x2d, attn_norm_w, w_in, tm=min(T, 1024), tn=512)
    attn_o = _attention(proj, attn_sinks, q_norm_w, k_norm_w, B, S)
    dn_o = _deltanet(proj, conv_w, a_log, dt_bias, dn_norm_w, B, S)
    h, u, top_i, top_g = _outproj_router(attn_o, dn_o, x2d, w_out, ffn_norm_w, w_router, b_router,
                                         tm=min(T, 512))
    dest, row_tok, n_blocks_used, P, ie, ib, inb = _routing(top_i, T)
    y_sorted = _experts(u, row_tok, ie, ib, inb, n_blocks_used, w_gate_up, b_gate_up, w_down, b_down)
    out = _combine(h, top_g, y_sorted, dest, tm=min(T, 128))
    return out.reshape(B, S, D)


def kernel(x, attn_norm_w, w_in, q_norm_w, k_norm_w, attn_sinks, conv_w, a_log, dt_bias, dn_norm_w,
           w_out, ffn_norm_w, w_router, b_router, w_gate_up, b_gate_up, w_down, b_down):
    h = x
    for l in range(attn_norm_w.shape[0]):
        h = _layer(h, attn_norm_w[l], w_in[l], q_norm_w[l], k_norm_w[l], attn_sinks[l], conv_w[l],
                   a_log[l], dt_bias[l], dn_norm_w[l], w_out[l], ffn_norm_w[l], w_router[l],
                   b_router[l], w_gate_up[l], b_gate_up[l], w_down[l], b_down[l])
    return h
```

```python
import functools
import math

import jax
import jax.numpy as jnp
from jax import lax
from jax.experimental import pallas as pl
from jax.experimental.pallas import tpu as pltpu

F32 = jnp.float32
BF16 = jnp.bfloat16

D_MODEL = 2048
ATTN_HEAD_DIM = 64
ATTN_HEADS = 16
ATTN_KV_HEADS = 4
ATTN_GROUP = ATTN_HEADS // ATTN_KV_HEADS
WINDOW = 128
DN_HEAD_DIM = 128
DN_HEADS = 8
DN_CHUNK = 64
CONV_WIDTH = 4
ATTN_Q_W = ATTN_HEADS * ATTN_HEAD_DIM
ATTN_KV_W = ATTN_KV_HEADS * ATTN_HEAD_DIM
DN_W = DN_HEADS * DN_HEAD_DIM
IN_WIDTH = ATTN_Q_W + 2 * ATTN_KV_W + 4 * DN_W + 2 * DN_HEADS
N_EXPERTS = 32
TOP_K = 4
D_EXPERT = 2048
SWIGLU_ALPHA = 1.702
SWIGLU_LIMIT = 7.0
NORM_EPS = 1e-5
QK_NORM_EPS = 1e-6
DN_NORM_EPS = 1e-6
L2_EPS = 1e-6

LANES = 128
MIB = 1024 * 1024
HIGHEST = lax.Precision.HIGHEST

ATTN_K_BLK = ATTN_Q_W // ATTN_KV_W
ATTN_V_BLK = ATTN_K_BLK + 1
DN_Q_BLK = (ATTN_Q_W + 2 * ATTN_KV_W) // DN_HEAD_DIM
DN_K_BLK = DN_Q_BLK + DN_HEADS
DN_V_BLK = DN_K_BLK + DN_HEADS
DN_Z_BLK = DN_V_BLK + DN_HEADS
DN_BA_BLK = DN_Z_BLK + DN_HEADS

MOE_ROWS = 256
MOE_ITEM_BLOCKS = 9
MOE_F_TILE = 256


def _dot(a, b, precision=None):
    return jnp.dot(a, b, preferred_element_type=F32, precision=precision)


def _dot_nt(a, b, precision=None):
    return lax.dot_general(a, b, (((1,), (1,)), ((), ())), preferred_element_type=F32,
                           precision=precision)


def _dot_tn(a, b, precision=None):
    return lax.dot_general(a, b, (((0,), (0,)), ((), ())), preferred_element_type=F32,
                           precision=precision)


def _inproj_kernel(x_ref, nw_ref, w_ref, o_ref, u_ref, *, last_width):
    @pl.when(pl.program_id(1) == 0)
    def _():
        x = x_ref[...]
        ms = jnp.mean(x * x, axis=-1, keepdims=True)
        u_ref[...] = (x * lax.rsqrt(ms + NORM_EPS) * nw_ref[...]).astype(BF16)

    last = pl.num_programs(1) - 1

    @pl.when(pl.program_id(1) < last)
    def _():
        o_ref[...] = _dot(u_ref[...], w_ref[...].astype(BF16))

    @pl.when(pl.program_id(1) == last)
    def _():
        o_ref[:, :last_width] = _dot(u_ref[...], w_ref[:, :last_width].astype(BF16))


def _inproj(x2d, norm_w, w_in, tm, tn):
    T, D = x2d.shape
    N = w_in.shape[1]
    n_col_blocks = pl.cdiv(N, tn)
    last_width = min(tn, -(-(N - (n_col_blocks - 1) * tn) // LANES) * LANES)
    return pl.pallas_call(
        functools.partial(_inproj_kernel, last_width=last_width),
        out_shape=jax.ShapeDtypeStruct((T, N), F32),
        grid=(T // tm, pl.cdiv(N, tn)),
        in_specs=[
            pl.BlockSpec((tm, D), lambda i, j: (i, 0)),
            pl.BlockSpec((1, D), lambda i, j: (0, 0)),
            pl.BlockSpec((D, tn), lambda i, j: (0, j)),
        ],
        out_specs=pl.BlockSpec((tm, tn), lambda i, j: (i, j)),
        scratch_shapes=[pltpu.VMEM((tm, D), BF16)],
        compiler_params=pltpu.CompilerParams(
            dimension_semantics=("arbitrary", "arbitrary"), vmem_limit_bytes=56 * MIB),
        name="inproj",
    )(x2d, norm_w.reshape(1, D), w_in)


def _attn_kernel(sink_ref, q_ref, kp_ref, kc_ref, vp_ref, vc_ref, qw_ref, kw_ref, o_ref):
    n = pl.program_id(1)
    L = WINDOW
    d = ATTN_HEAD_DIM

    def head_norm(t, w):
        ms = jnp.mean(t * t, axis=-1, keepdims=True)
        return t * lax.rsqrt(ms + QK_NORM_EPS) * w

    G = ATTN_GROUP
    qi = lax.broadcasted_iota(jnp.int32, (G * L, 2 * L), 0)
    kj = lax.broadcasted_iota(jnp.int32, (G * L, 2 * L), 1)
    dist = (qi & (L - 1)) + L - kj
    valid = (dist >= 0) & (dist < WINDOW) & ((kj >= L) | (n > 0))
    dist_f = dist.astype(F32)
    grp = lax.broadcasted_iota(jnp.int32, (G * L, 1), 0) >> (L.bit_length() - 1)

    def per_row(vals):
        col = jnp.full((G * L, 1), vals[G - 1], F32)
        for g in range(G - 2, -1, -1):
            col = jnp.where(grp == g, vals[g], col)
        return col

    qw = qw_ref[...]
    kw = kw_ref[...]
    kvs = range(ATTN_KV_HEADS)
    heads = lambda kv: [kv * G + g for g in range(G)]
    cols = lambda h: slice(h * d, (h + 1) * d)
    k = [head_norm(jnp.concatenate([kp_ref[:, cols(kv)], kc_ref[:, cols(kv)]], axis=0), kw).astype(BF16)
         for kv in kvs]
    v = [jnp.concatenate([vp_ref[:, cols(kv)], vc_ref[:, cols(kv)]], axis=0).astype(BF16) for kv in kvs]
    q = [head_norm(jnp.concatenate([q_ref[:, cols(h)] for h in heads(kv)], axis=0), qw).astype(BF16)
         for kv in kvs]
    s = [_dot_nt(q[kv], k[kv]) * (d ** -0.5) for kv in kvs]
    slope = [per_row([2.0 ** (-8.0 * (h + 1) / ATTN_HEADS) for h in heads(kv)]) for kv in kvs]
    sink = [per_row([sink_ref[h] for h in heads(kv)]) for kv in kvs]
    s = [jnp.where(valid, s[kv] - slope[kv] * dist_f, -jnp.inf) for kv in kvs]
    m = [jnp.maximum(jnp.max(s[kv], axis=-1, keepdims=True), sink[kv]) for kv in kvs]
    p = [jnp.exp(s[kv] - m[kv]) for kv in kvs]
    denom = [jnp.sum(p[kv], axis=-1, keepdims=True) + jnp.exp(sink[kv] - m[kv]) for kv in kvs]
    probs = [(p[kv] / denom[kv]).astype(BF16) for kv in kvs]
    o = [_dot(probs[kv], v[kv]) for kv in kvs]
    for kv in kvs:
        for g, h in enumerate(heads(kv)):
            o_ref[:, cols(h)] = o[kv][g * L:(g + 1) * L, :].astype(o_ref.dtype)


def _attention(proj, sinks, q_norm_w, k_norm_w, B, S):
    nb = S // WINDOW
    L = WINDOW
    row = lambda b, n: b * nb + n
    prev = lambda b, n: b * nb + jnp.maximum(n - 1, 0)
    return pl.pallas_call(
        _attn_kernel,
        out_shape=jax.ShapeDtypeStruct((B * S, ATTN_Q_W), BF16),
        grid=(B, nb),
        in_specs=[
            pl.BlockSpec(memory_space=pltpu.SMEM),
            pl.BlockSpec((L, ATTN_Q_W), lambda b, n: (row(b, n), 0)),
            pl.BlockSpec((L, ATTN_KV_W), lambda b, n: (prev(b, n), ATTN_K_BLK)),
            pl.BlockSpec((L, ATTN_KV_W), lambda b, n: (row(b, n), ATTN_K_BLK)),
            pl.BlockSpec((L, ATTN_KV_W), lambda b, n: (prev(b, n), ATTN_V_BLK)),
            pl.BlockSpec((L, ATTN_KV_W), lambda b, n: (row(b, n), ATTN_V_BLK)),
            pl.BlockSpec((1, ATTN_HEAD_DIM), lambda b, n: (0, 0)),
            pl.BlockSpec((1, ATTN_HEAD_DIM), lambda b, n: (0, 0)),
        ],
        out_specs=pl.BlockSpec((L, ATTN_Q_W), lambda b, n: (row(b, n), 0)),
        compiler_params=pltpu.CompilerParams(dimension_semantics=("arbitrary", "arbitrary")),
        name="attn",
    )(sinks, proj, proj, proj, proj, proj, q_norm_w.reshape(1, -1), k_norm_w.reshape(1, -1))


CONV_PAD = 8


DN_GROUP = 16

_X3 = ((0, 0), (1, 0), (0, 1))
_EXACT_LHS = ((0, 0), (0, 1), (0, 2))
_EXACT_RHS = ((0, 0), (1, 0), (2, 0))


def _split(x, n):
    parts = []
    for i in range(n):
        p = x.astype(BF16)
        parts.append(p)
        if i + 1 < n:
            x = x - p.astype(F32)
    return parts


def _mm(a_parts, b_parts, terms, form='nn'):
    a_axis = 0 if form == 'tn' else 1
    b_axis = 1 if form == 'nt' else 0
    a = jnp.concatenate([a_parts[i] for i, _ in terms], axis=a_axis)
    b = jnp.concatenate([b_parts[j] for _, j in terms], axis=b_axis)
    return {'nn': _dot, 'nt': _dot_nt, 'tn': _dot_tn}[form](a, b)


def _dn_kernel(alog_ref, dtb_ref, q_ref, k_ref, v_ref, z_ref, ba_ref, cq_ref, ck_ref, cv_ref,
               nw_ref, o_ref, xp, qs, ks, vs, gs, bs, us, ws, am, pm, qm, gl):
    h = pl.program_id(1)
    S = q_ref.shape[0]
    C = DN_CHUNK
    dk = DN_HEAD_DIM
    piece = min(S, 256)

    def conv_silu(x_ref, cw_ref, dst, l2):
        xp[0:CONV_PAD, :] = jnp.zeros((CONV_PAD, dk), F32)
        xp[CONV_PAD:CONV_PAD + S, :] = x_ref[...]
        for p in range(S // piece):
            r0 = p * piece
            acc = jnp.zeros((piece, dk), F32)
            for i in range(CONV_WIDTH):
                off = CONV_PAD + r0 - (CONV_WIDTH - 1) + i
                acc = acc + xp[off:off + piece, :] * cw_ref[i:i + 1, :]
            y = acc * jax.nn.sigmoid(acc)
            if l2:
                y = y * lax.rsqrt(jnp.sum(y * y, axis=-1, keepdims=True) + L2_EPS)
            dst[r0:r0 + piece, :] = y

    conv_silu(q_ref, cq_ref, qs, True)
    conv_silu(k_ref, ck_ref, ks, True)
    conv_silu(v_ref, cv_ref, vs, False)

    neg_a = -jnp.exp(jnp.full((1, dk), alog_ref[h], F32))
    dtb = dtb_ref[h]
    for p in range(S // piece):
        r0 = p * piece
        ba = ba_ref[r0:r0 + piece, :]
        lane = lax.broadcasted_iota(jnp.int32, ba.shape, 1)
        b_col = jnp.sum(jnp.where(lane == h, ba, 0.0), axis=-1, keepdims=True)
        a_col = jnp.sum(jnp.where(lane == DN_HEADS + h, ba, 0.0), axis=-1, keepdims=True)
        bs[r0:r0 + piece, :] = jnp.broadcast_to(jax.nn.sigmoid(b_col), (piece, dk))
        gs[r0:r0 + piece, :] = neg_a * jnp.broadcast_to(jax.nn.softplus(a_col + dtb), (piece, dk))

    ri = lax.broadcasted_iota(jnp.int32, (C, C), 0)
    ci = lax.broadcasted_iota(jnp.int32, (C, C), 1)
    incl = ri >= ci
    strict = ri > ci
    tri_ones = [jnp.where(incl, 1.0, 0.0).astype(BF16)]
    upper_ones = [jnp.where(ri <= ci, 1.0, 0.0).astype(BF16)]
    nw = nw_ref[...]
    n_chunks = S // C
    group = DN_GROUP if n_chunks % DN_GROUP == 0 else 1

    def intra_load(c):
        rows = pl.ds(pl.multiple_of(c * C, C), C)
        return qs[rows, :], ks[rows, :], vs[rows, :], gs[rows, :], bs[rows, :]

    def intra_compute(loaded):
        each = lambda f, *ls: [f(*a) for a in zip(*ls)]
        q, k, v, g, beta = (list(t) for t in zip(*loaded))
        q = each(lambda t: t * (dk ** -0.5), q)
        g3 = each(lambda t: _split(t, 3), g)
        gc = each(lambda p: _mm(tri_ones, p, _EXACT_LHS), g3)
        gc_row = each(lambda p: _mm([t[:, :C] for t in p], upper_ones, _EXACT_RHS, 'tn'), g3)
        decay = each(lambda a, b: jnp.exp(jnp.where(incl, a[:, :C] - b, -jnp.inf)), gc, gc_row)
        kb = each(jnp.multiply, k, beta)
        k2 = each(lambda t: _split(t, 2), k)
        kk = each(lambda a, b: _mm(_split(a, 2), b, _X3, 'nt'), kb, k2)
        qk = each(lambda a, b: _mm(_split(a, 2), b, _X3, 'nt'), q, k2)
        m = each(lambda a, d: -jnp.where(strict, a * d, 0.0), kk, decay)
        x = each(lambda vv, bb, kbb, gg: jnp.concatenate([vv * bb, kbb * jnp.exp(gg)], axis=1),
                 v, beta, kb, gc)
        n_fac = int(math.log2(C))
        for it in range(n_fac):
            m2 = each(lambda t: _split(t, 2), m)
            x = each(lambda xx, mm: xx + _mm(mm, _split(xx, 2), _X3), x, m2)
            if it + 1 < n_fac:
                m = each(lambda mm: _mm(mm, mm, _X3), m2)
        kd2 = each(lambda kk_, gg: _split(kk_ * jnp.exp(gg[C - 1:C, :] - gg), 2), k, gc)
        pq = each(lambda a, xx: _mm(a, _split(xx, 2), _X3, 'tn'), kd2, x)
        a_intra = each(lambda a, d: jnp.where(incl, a * d, 0.0), qk, decay)
        qg = each(lambda a, gg: a * jnp.exp(gg), q, gc)
        e_last = each(lambda gg: jnp.broadcast_to(jnp.exp(gg[C - 1:C, :]), (8, dk)), gc)
        return list(zip(x, qg, a_intra, pq, e_last))

    def intra_store(c, x, qg, a_intra, pq, e_last):
        rows = pl.ds(pl.multiple_of(c * C, C), C)
        prow = pl.ds(pl.multiple_of(c * dk, dk), dk)
        us[rows, :] = x[:, :dk]
        ws[rows, :] = x[:, dk:]
        qs[rows, :] = qg
        am[rows, :] = a_intra
        qm[prow, :] = pq[:, :dk]
        pm[prow, :] = pq[:, dk:]
        gl[pl.ds(pl.multiple_of(c * 8, 8), 8), :] = e_last

    def intra_group(i, _):
        loaded = [intra_load(i * group + t) for t in range(group)]
        outs = intra_compute(loaded)
        for t in range(group):
            intra_store(i * group + t, *outs[t])
        return 0

    lax.fori_loop(0, n_chunks // group, intra_group, 0)

    def scan(c, state):
        prow = pl.ds(pl.multiple_of(c * dk, dk), dk)
        ps = _mm(_split(pm[prow, :], 2), _split(state, 2), _X3)
        pm[prow, :] = state
        e_last = gl[pl.ds(pl.multiple_of(c * 8, 8), 1), :]
        return state * e_last - ps + qm[prow, :]

    lax.fori_loop(0, n_chunks, scan, jnp.zeros((dk, dk), F32), unroll=2)

    def out_group(i, _):
        cs = [i * group + t for t in range(group)]
        rows = [pl.ds(pl.multiple_of(c * C, C), C) for c in cs]
        each = lambda f, *ls: [f(*a) for a in zip(*ls)]
        state2 = each(lambda c: _split(pm[pl.ds(pl.multiple_of(c * dk, dk), dk), :], 2), cs)
        lhs2 = each(lambda r: _split(jnp.concatenate([ws[r, :], qs[r, :]], axis=0), 2), rows)
        prod = each(lambda a, b: _mm(a, b, _X3), lhs2, state2)
        v_new = each(lambda r, p: us[r, :] - p[:C], rows, prod)
        av = each(lambda r, vn: _mm(_split(am[r, :], 2), _split(vn, 2), _X3), rows, v_new)
        o = each(lambda p, a: p[C:] + a, prod, av)
        o = each(lambda t: t * lax.rsqrt(jnp.mean(t * t, axis=-1, keepdims=True) + DN_NORM_EPS) * nw, o)
        for r, t in zip(rows, o):
            z = z_ref[r, :]
            o_ref[r, :] = (t * (z * jax.nn.sigmoid(z))).astype(o_ref.dtype)
        return 0

    lax.fori_loop(0, n_chunks // group, out_group, 0)


def _deltanet(proj, conv_w, a_log, dt_bias, dn_norm_w, B, S):
    dk = DN_HEAD_DIM
    seq = lambda blk: pl.BlockSpec((S, dk), lambda b, h: (b, blk + h))
    cw = lambda blk: pl.BlockSpec((CONV_WIDTH, dk), lambda b, h: (0, blk + h))
    return pl.pallas_call(
        _dn_kernel,
        out_shape=jax.ShapeDtypeStruct((B * S, DN_W), BF16),
        grid=(B, DN_HEADS),
        in_specs=[
            pl.BlockSpec(memory_space=pltpu.SMEM),
            pl.BlockSpec(memory_space=pltpu.SMEM),
            seq(DN_Q_BLK), seq(DN_K_BLK), seq(DN_V_BLK), seq(DN_Z_BLK),
            pl.BlockSpec((S, LANES), lambda b, h: (b, DN_BA_BLK)),
            cw(0), cw(DN_HEADS), cw(2 * DN_HEADS),
            pl.BlockSpec((1, dk), lambda b, h: (0, 0)),
        ],
        out_specs=pl.BlockSpec((S, dk), lambda b, h: (b, h)),
        scratch_shapes=[pltpu.VMEM((S + CONV_PAD, dk), F32)] + [pltpu.VMEM((S, dk), F32)] * 7
        + [pltpu.VMEM((S, DN_CHUNK), F32)] + [pltpu.VMEM((S // DN_CHUNK * dk, dk), F32)] * 2
        + [pltpu.VMEM((S // DN_CHUNK * 8, dk), F32)],
        compiler_params=pltpu.CompilerParams(
            dimension_semantics=("arbitrary", "arbitrary"), vmem_limit_bytes=40 * MIB),
        name="deltanet",
    )(a_log, dt_bias, proj, proj, proj, proj, proj, conv_w, conv_w, conv_w,
      dn_norm_w.reshape(1, dk))


def _pack_bf16_pairs(x):
    c = x.shape[1] // 2
    bits = lambda t: lax.bitcast_convert_type(t.astype(BF16).astype(F32), jnp.uint32)
    return (bits(x[:, :c]) >> 16) | (bits(x[:, c:]) & jnp.uint32(0xFFFF0000))


def _unpack_bf16_pairs(p):
    lo = lax.bitcast_convert_type(p << 16, F32).astype(BF16)
    hi = lax.bitcast_convert_type(p & jnp.uint32(0xFFFF0000), F32).astype(BF16)
    return lo, hi


def _outproj_kernel(a_ref, d_ref, x_ref, wa_ref, wd_ref, nw_ref, wrh_ref, wrl_ref, br_ref,
                    h_ref, u_ref, ti_ref, tg_ref):
    h = x_ref[...] + _dot(a_ref[...], wa_ref[...]) + _dot(d_ref[...], wd_ref[...])
    h_ref[...] = h
    ms = jnp.mean(h * h, axis=-1, keepdims=True)
    u = h * lax.rsqrt(ms + NORM_EPS) * nw_ref[...]
    u_ref[...] = _pack_bf16_pairs(u)
    u_hi, u_lo = _split(u, 2)
    logits = (_dot(u_hi, wrh_ref[...]) + _dot(u_lo, wrh_ref[...]) + _dot(u_hi, wrl_ref[...])
              + br_ref[...])
    lane = lax.broadcasted_iota(jnp.int32, logits.shape, 1)
    lane_f = lane.astype(F32)
    vals = jnp.where(lane < N_EXPERTS, logits, -jnp.inf)
    top_v, top_i = [], []
    for _ in range(TOP_K):
        m = jnp.max(vals, axis=-1, keepdims=True)
        idx = jnp.min(jnp.where(vals == m, lane_f, float(LANES)), axis=-1, keepdims=True)
        top_v.append(m)
        top_i.append(idx)
        vals = jnp.where(lane_f == idx, -jnp.inf, vals)
    e = [jnp.exp(v - top_v[0]) for v in top_v]
    denom = e[0] + e[1] + e[2] + e[3]
    ti = jnp.zeros(logits.shape, F32)
    tg = jnp.zeros(logits.shape, F32)
    for k in range(TOP_K):
        ti = jnp.where(lane == k, top_i[k], ti)
        tg = jnp.where(lane == k, e[k] / denom, tg)
    ti_ref[...] = ti.astype(jnp.int32)
    tg_ref[...] = tg


def _outproj_router(attn_o, dn_o, x2d, w_out, ffn_norm_w, w_router, b_router, tm):
    T, D = x2d.shape
    wa = w_out[:ATTN_Q_W].astype(BF16)
    wd = w_out[ATTN_Q_W:].astype(BF16)
    wr = jnp.zeros((D, LANES), F32).at[:, :N_EXPERTS].set(w_router)
    wr_hi = wr.astype(BF16)
    wr_lo = (wr - wr_hi.astype(F32)).astype(BF16)
    br = jnp.zeros((1, LANES), F32).at[0, :N_EXPERTS].set(b_router)
    const = lambda shape: pl.BlockSpec(shape, lambda i: (0, 0))
    tile = lambda w: pl.BlockSpec((tm, w), lambda i: (i, 0))
    return pl.pallas_call(
        _outproj_kernel,
        out_shape=(jax.ShapeDtypeStruct((T, D), F32), jax.ShapeDtypeStruct((T, D // 2), jnp.uint32),
                   jax.ShapeDtypeStruct((T, LANES), jnp.int32),
                   jax.ShapeDtypeStruct((T, LANES), F32)),
        grid=(T // tm,),
        in_specs=[tile(ATTN_Q_W), tile(DN_W), tile(D), const((ATTN_Q_W, D)), const((DN_W, D)),
                  const((1, D)), const((D, LANES)), const((D, LANES)), const((1, LANES))],
        out_specs=(tile(D), tile(D // 2), tile(LANES), tile(LANES)),
        compiler_params=pltpu.CompilerParams(
            dimension_semantics=("arbitrary",), vmem_limit_bytes=56 * MIB),
        name="outproj",
    )(attn_o, dn_o, x2d, wa, wd, ffn_norm_w.reshape(1, D), wr_hi, wr_lo, br)


def _experts_kernel(ie_ref, ib_ref, in_ref, used_ref, x_ref, wg_ref, wl_ref, wd_ref,
                    bg_ref, bl_ref, bd_ref, y_ref, xraw, ybuf, wg_s, wl_s, wd_s, sem_x, sem_y):
    w = pl.program_id(0)
    j = pl.program_id(1)
    nw = pl.num_programs(0)
    nj = pl.num_programs(1)
    R = MOE_ROWS
    nb = in_ref[w]
    row0 = ib_ref[w] * R
    slot = lax.rem(w, 2)
    w_next = jnp.minimum(w + 1, nw - 1)
    nb_next = jnp.where(w + 1 < nw, in_ref[w_next], 0)
    row0_next = ib_ref[w_next] * R

    def x_copy(base, s, r):
        return pltpu.make_async_copy(x_ref.at[pl.ds(base + r * R, R), :],
                                     xraw.at[s, pl.ds(r * R, R), :], sem_x.at[s])

    def y_copy(r):
        return pltpu.make_async_copy(ybuf.at[pl.ds(r * R, R), :],
                                     y_ref.at[pl.ds(row0 + r * R, R), :], sem_y)

    def for_range(n, fn):
        def body(r, _):
            fn(r)
            return 0
        lax.fori_loop(0, n, body, 0)

    def rows_of(r):
        return pl.ds(pl.multiple_of(r * R, R), R)

    @pl.when((j == 0) & (w == 0))
    def _():
        for_range(nb, lambda r: x_copy(row0, slot, r).start())

    @pl.when(j == 0)
    def _():
        bd = jnp.broadcast_to(bd_ref[...], (R, ybuf.shape[1]))

        def init(r):
            ybuf[rows_of(r), :] = bd

        for_range(nb, init)
        for_range(nb, lambda r: x_copy(row0, slot, r).wait())
        for_range(nb_next, lambda r: x_copy(row0_next, 1 - slot, r).start())

    @pl.when(nb > 0)
    def _():
        wg_s[...] = wg_ref[...].astype(BF16)
        wl_s[...] = wl_ref[...].astype(BF16)
        wd_s[...] = wd_ref[...].astype(BF16)

    bg = bg_ref[...]
    bl = bl_ref[...]

    def blocks(rs, write_back):
        rows = [rows_of(r) for r in rs]
        xs = [jnp.concatenate(_unpack_bf16_pairs(xraw[slot, rw, :]), axis=1) for rw in rows]
        hg = [jnp.minimum(_dot(x, wg_s[...]) + bg, SWIGLU_LIMIT) for x in xs]
        hl = [jnp.clip(_dot(x, wl_s[...]) + bl, -SWIGLU_LIMIT, SWIGLU_LIMIT) for x in xs]
        acts = [(a * jax.nn.sigmoid(SWIGLU_ALPHA * a) * (b + 1.0)).astype(BF16) for a, b in zip(hg, hl)]
        for rw, act in zip(rows, acts):
            ybuf[rw, :] += _dot(act, wd_s[...])
        if write_back:
            for r in rs:
                y_copy(r).start()

    def all_blocks(write_back):
        def pair(i, _):
            blocks([2 * i, 2 * i + 1], write_back)
            return 0

        lax.fori_loop(0, nb // 2, pair, 0)

        @pl.when(nb % 2 == 1)
        def _():
            blocks([nb - 1], write_back)

    @pl.when(j < nj - 1)
    def _():
        all_blocks(False)

    @pl.when(j == nj - 1)
    def _():
        all_blocks(True)
        for_range(nb, lambda r: y_copy(r).wait())

    @pl.when((j == nj - 1) & (w == pl.num_programs(0) - 1))
    def _():
        ybuf[0:R, :] = jnp.zeros((R, ybuf.shape[1]), F32)

        def tail_copy(r):
            return pltpu.make_async_copy(ybuf.at[pl.ds(0, R), :], y_ref.at[pl.ds(r * R, R), :], sem_y)

        def tail(fn):
            def body(r, _):
                fn(r)
                return 0
            lax.fori_loop(used_ref[0], y_ref.shape[0] // R, body, 0)

        tail(lambda r: tail_copy(r).start())
        tail(lambda r: tail_copy(r).wait())


def _experts(x_sorted, item_expert, item_blk0, item_nblk, n_blocks_used,
             w_gate_up, b_gate_up, w_down, b_down):
    P = x_sorted.shape[0]
    D = x_sorted.shape[1] * 2
    E, _, F2 = w_gate_up.shape
    F = F2 // 2
    tf = MOE_F_TILE
    nj = F // tf
    n_items = item_expert.shape[0]
    rows = MOE_ITEM_BLOCKS * MOE_ROWS
    b_gu = b_gate_up.reshape(E, 1, F2)
    b_d = b_down.reshape(E, 1, D)
    jt = lambda j, n: jnp.where(n > 0, j, nj - 1)
    return pl.pallas_call(
        _experts_kernel,
        out_shape=jax.ShapeDtypeStruct((P, D), F32),
        grid_spec=pltpu.PrefetchScalarGridSpec(
            num_scalar_prefetch=4,
            grid=(n_items, nj),
            in_specs=[
                pl.BlockSpec(memory_space=pl.ANY),
                pl.BlockSpec((None, D, tf), lambda w, j, ie, ib, nb, *_: (ie[w], 0, jt(j, nb[w]))),
                pl.BlockSpec((None, D, tf), lambda w, j, ie, ib, nb, *_: (ie[w], 0, nj + jt(j, nb[w]))),
                pl.BlockSpec((None, tf, D), lambda w, j, ie, ib, nb, *_: (ie[w], jt(j, nb[w]), 0)),
                pl.BlockSpec((None, 1, tf), lambda w, j, ie, ib, nb, *_: (ie[w], 0, jt(j, nb[w]))),
                pl.BlockSpec((None, 1, tf), lambda w, j, ie, ib, nb, *_: (ie[w], 0, nj + jt(j, nb[w]))),
                pl.BlockSpec((None, 1, D), lambda w, j, ie, *_: (ie[w], 0, 0)),
            ],
            out_specs=pl.BlockSpec(memory_space=pl.ANY),
            scratch_shapes=[
                pltpu.VMEM((2, rows, D // 2), jnp.uint32), pltpu.VMEM((rows, D), F32),
                pltpu.VMEM((D, tf), BF16), pltpu.VMEM((D, tf), BF16), pltpu.VMEM((tf, D), BF16),
                pltpu.SemaphoreType.DMA((2,)), pltpu.SemaphoreType.DMA,
            ],
        ),
        compiler_params=pltpu.CompilerParams(
            dimension_semantics=("arbitrary", "arbitrary"), vmem_limit_bytes=60 * MIB),
        name="experts",
    )(item_expert, item_blk0, item_nblk, n_blocks_used, x_sorted,
      w_gate_up, w_gate_up, w_down, b_gu, b_gu, b_d)


def _combine_kernel(dest_ref, h_ref, tg_ref, y_ref, o_ref, buf, sem):
    i = pl.program_id(0)
    n = pl.num_programs(0)
    tm = h_ref.shape[0]

    def issue_tile(step):
        slot = lax.rem(step, 2)

        def issue(r, _):
            for t in range(2):
                for k in range(TOP_K):
                    row = dest_ref[(step * tm + 2 * r + t) * TOP_K + k]
                    pltpu.make_async_copy(y_ref.at[pl.ds(row, 1), :],
                                          buf.at[slot, k, pl.ds(2 * r + t, 1), :], sem.at[slot]).start()
            return 0

        lax.fori_loop(0, tm // 2, issue, 0)

    @pl.when(i == 0)
    def _():
        issue_tile(0)

    @pl.when(i + 1 < n)
    def _():
        issue_tile(i + 1)

    slot = lax.rem(i, 2)
    for k in range(TOP_K):
        pltpu.make_async_copy(y_ref.at[pl.ds(0, tm), :], buf.at[slot, k], sem.at[slot]).wait()
    tg = tg_ref[...]
    acc = h_ref[...]
    for k in range(TOP_K):
        acc = acc + tg[:, k:k + 1] * buf[slot, k]
    o_ref[...] = acc


def _combine(h, top_g, y_sorted, dest, tm):
    T, D = h.shape
    return pl.pallas_call(
        _combine_kernel,
        out_shape=jax.ShapeDtypeStruct((T, D), F32),
        grid_spec=pltpu.PrefetchScalarGridSpec(
            num_scalar_prefetch=1,
            grid=(T // tm,),
            in_specs=[
                pl.BlockSpec((tm, D), lambda i, d: (i, 0)),
                pl.BlockSpec((tm, LANES), lambda i, d: (i, 0)),
                pl.BlockSpec(memory_space=pl.ANY),
            ],
            out_specs=pl.BlockSpec((tm, D), lambda i, d: (i, 0)),
            scratch_shapes=[pltpu.VMEM((2, TOP_K, tm, D), F32), pltpu.SemaphoreType.DMA((2,))],
        ),
        compiler_params=pltpu.CompilerParams(dimension_semantics=("arbitrary",)),
        name="combine",
    )(dest, h, top_g, y_sorted)


def _dispatch_kernel(dest_ref, row_end_ref, blk_end_ref, u_ref, x_ref, buf, zbuf, sem_l, sem_s, sem_z):
    i = pl.program_id(0)
    n = pl.num_programs(0)
    tm = buf.shape[1]
    R = MOE_ROWS
    n_blocks = x_ref.shape[0] // R

    def load(step):
        s = lax.rem(step, 2)
        return pltpu.make_async_copy(u_ref.at[pl.ds(step * tm, tm), :], buf.at[s], sem_l.at[s])

    def zero_copy(blk):
        return pltpu.make_async_copy(zbuf, x_ref.at[pl.ds(blk * R, R), :], sem_z)

    def wait_scatter(step):
        s = lax.rem(step, 2)
        for _ in range(TOP_K):
            pltpu.make_async_copy(u_ref.at[pl.ds(0, tm), :], buf.at[s], sem_s.at[s]).wait()

    @pl.when(i == 0)
    def _():
        zbuf[...] = jnp.zeros(zbuf.shape, zbuf.dtype)

        def partial_blocks(fn):
            def body(e, _):
                @pl.when(lax.rem(row_end_ref[e], R) != 0)
                def _():
                    fn(blk_end_ref[e] - 1)
                return 0
            lax.fori_loop(0, N_EXPERTS, body, 0)

        def unused_blocks(fn):
            def body(b, _):
                fn(b)
                return 0
            lax.fori_loop(blk_end_ref[N_EXPERTS - 1], n_blocks, body, 0)

        partial_blocks(lambda b: zero_copy(b).start())
        unused_blocks(lambda b: zero_copy(b).start())
        partial_blocks(lambda b: zero_copy(b).wait())
        unused_blocks(lambda b: zero_copy(b).wait())
        load(0).start()

    @pl.when(i > 0)
    def _():
        wait_scatter(i - 1)

    @pl.when(i + 1 < n)
    def _():
        load(i + 1).start()

    load(i).wait()
    slot = lax.rem(i, 2)

    def issue(r, _):
        for t in range(2):
            for k in range(TOP_K):
                row = dest_ref[(i * tm + 2 * r + t) * TOP_K + k]
                pltpu.make_async_copy(buf.at[slot, pl.ds(2 * r + t, 1), :], x_ref.at[pl.ds(row, 1), :],
                                      sem_s.at[slot]).start()
        return 0

    lax.fori_loop(0, tm // 2, issue, 0)

    @pl.when(i == n - 1)
    def _():
        wait_scatter(i)


def _dispatch(u_packed, dest, row_end, blk_end, P, tm):
    T, W = u_packed.shape
    return pl.pallas_call(
        _dispatch_kernel,
        out_shape=jax.ShapeDtypeStruct((P, W), u_packed.dtype),
        grid_spec=pltpu.PrefetchScalarGridSpec(
            num_scalar_prefetch=3,
            grid=(T // tm,),
            in_specs=[pl.BlockSpec(memory_space=pl.ANY)],
            out_specs=pl.BlockSpec(memory_space=pl.ANY),
            scratch_shapes=[pltpu.VMEM((2, tm, W), u_packed.dtype), pltpu.VMEM((MOE_ROWS, W), u_packed.dtype),
                            pltpu.SemaphoreType.DMA((2,)), pltpu.SemaphoreType.DMA((2,)),
                            pltpu.SemaphoreType.DMA],
        ),
        compiler_params=pltpu.CompilerParams(dimension_semantics=("arbitrary",)),
        name="dispatch",
    )(dest, row_end, blk_end, u_packed)


def _routing(top_i, T):
    R = MOE_ROWS
    n_assign = T * TOP_K
    e_flat = top_i[:, :TOP_K].reshape(n_assign)
    onehot = (e_flat[:, None] == jnp.arange(N_EXPERTS, dtype=jnp.int32)[None, :]).astype(jnp.int32)
    csum = jnp.cumsum(onehot, axis=0)
    rank = jnp.take_along_axis(csum, e_flat[:, None], axis=1)[:, 0] - 1
    counts = csum[-1]
    nblk = (counts + R - 1) // R
    blk_end = jnp.cumsum(nblk)
    blk_start = blk_end - nblk
    dest = blk_start[e_flat] * R + rank
    max_blocks = n_assign // R + N_EXPERTS
    P = max_blocks * R
    dest = dest.astype(jnp.int32)
    row_end = (blk_start * R + counts).astype(jnp.int32)
    nb_item = MOE_ITEM_BLOCKS
    n_items_e = (nblk + nb_item - 1) // nb_item
    item_end = jnp.cumsum(n_items_e)
    item_start = item_end - n_items_e
    max_items = N_EXPERTS + -(-max_blocks // nb_item)
    w = jnp.arange(max_items, dtype=jnp.int32)
    ie = jnp.minimum(jnp.searchsorted(item_end, w, side='right'), N_EXPERTS - 1).astype(jnp.int32)
    local = w - item_start[ie]
    live = w < item_end[-1]
    ib = blk_start[ie] + local * nb_item
    inb = jnp.clip(nblk[ie] - local * nb_item, 0, nb_item)
    last_e = ie[jnp.maximum(item_end[-1] - 1, 0)]
    ie = jnp.where(live, ie, last_e).astype(jnp.int32)
    ib = jnp.where(live, ib, 0).astype(jnp.int32)
    inb = jnp.where(live, inb, 0).astype(jnp.int32)
    return dest, row_end, blk_end.astype(jnp.int32), P, ie, ib, inb


def _layer(x, attn_norm_w, w_in, q_norm_w, k_norm_w, attn_sinks, conv_w, a_log, dt_bias,
           dn_norm_w, w_out, ffn_norm_w, w_router, b_router, w_gate_up, b_gate_up, w_down, b_down):
    B, S, D = x.shape
    T = B * S
    x2d = x.reshape(T, D)
    proj = _inproj(x2d, attn_norm_w, w_in, tm=min(T, 1024), tn=512)
    attn_o = _attention(proj, attn_sinks, q_norm_w, k_norm_w, B, S)
    dn_o = _deltanet(proj, conv_w, a_log, dt_bias, dn_norm_w, B, S)
    h, u, top_i, top_g = _outproj_router(attn_o, dn_o, x2d, w_out, ffn_norm_w, w_router, b_router,
                                         tm=min(T, 512))
    dest, row_end, blk_end, P, ie, ib, inb = _routing(top_i, T)
    x_sorted = _dispatch(u, dest, row_end, blk_end, P, tm=min(T, 256))
    y_sorted = _experts(x_sorted, ie, ib, inb, blk_end[-1:], w_gate_up, b_gate_up, w_down, b_down)
    out = _combine(h, top_g, y_sorted, dest, tm=min(T, 128))
    return out.reshape(B, S, D)


def kernel(x, attn_norm_w, w_in, q_norm_w, k_norm_w, attn_sinks, conv_w, a_log, dt_bias, dn_norm_w,
           w_out, ffn_norm_w, w_router, b_router, w_gate_up, b_gate_up, w_down, b_down):
    h = x
    for l in range(attn_norm_w.shape[0]):
        h = _layer(h, attn_norm_w[l], w_in[l], q_norm_w[l], k_norm_w[l], attn_sinks[l], conv_w[l],
                   a_log[l], dt_bias[l], dn_norm_w[l], w_out[l], ffn_norm_w[l], w_router[l],
                   b_router[l], w_gate_up[l], b_gate_up[l], w_down[l], b_down[l])
    return h
```

```python
import functools
import math

import jax
import jax.numpy as jnp
from jax import lax
from jax.experimental import pallas as pl
from jax.experimental.pallas import tpu as pltpu

F32 = jnp.float32
BF16 = jnp.bfloat16

D_MODEL = 2048
ATTN_HEAD_DIM = 64
ATTN_HEADS = 16
ATTN_KV_HEADS = 4
ATTN_GROUP = ATTN_HEADS // ATTN_KV_HEADS
WINDOW = 128
DN_HEAD_DIM = 128
DN_HEADS = 8
DN_CHUNK = 64
CONV_WIDTH = 4
ATTN_Q_W = ATTN_HEADS * ATTN_HEAD_DIM
ATTN_KV_W = ATTN_KV_HEADS * ATTN_HEAD_DIM
DN_W = DN_HEADS * DN_HEAD_DIM
IN_WIDTH = ATTN_Q_W + 2 * ATTN_KV_W + 4 * DN_W + 2 * DN_HEADS
N_EXPERTS = 32
TOP_K = 4
D_EXPERT = 2048
SWIGLU_ALPHA = 1.702
SWIGLU_LIMIT = 7.0
NORM_EPS = 1e-5
QK_NORM_EPS = 1e-6
DN_NORM_EPS = 1e-6
L2_EPS = 1e-6

LANES = 128
MIB = 1024 * 1024
HIGHEST = lax.Precision.HIGHEST

ATTN_K_BLK = ATTN_Q_W // ATTN_KV_W
ATTN_V_BLK = ATTN_K_BLK + 1
DN_Q_BLK = (ATTN_Q_W + 2 * ATTN_KV_W) // DN_HEAD_DIM
DN_K_BLK = DN_Q_BLK + DN_HEADS
DN_V_BLK = DN_K_BLK + DN_HEADS
DN_Z_BLK = DN_V_BLK + DN_HEADS
DN_BA_BLK = DN_Z_BLK + DN_HEADS

MOE_ROWS = 256
MOE_ITEM_BLOCKS = 9
MOE_F_TILE = 256
ROWS_PER_ISSUE = 4


def _dot(a, b, precision=None):
    return jnp.dot(a, b, preferred_element_type=F32, precision=precision)


def _dot_nt(a, b, precision=None):
    return lax.dot_general(a, b, (((1,), (1,)), ((), ())), preferred_element_type=F32,
                           precision=precision)


def _dot_tn(a, b, precision=None):
    return lax.dot_general(a, b, (((0,), (0,)), ((), ())), preferred_element_type=F32,
                           precision=precision)


def _inproj_kernel(x_ref, nw_ref, w_ref, o_ref, u_ref, *, last_width):
    @pl.when(pl.program_id(1) == 0)
    def _():
        x = x_ref[...]
        ms = jnp.mean(x * x, axis=-1, keepdims=True)
        u_ref[...] = (x * lax.rsqrt(ms + NORM_EPS) * nw_ref[...]).astype(BF16)

    last = pl.num_programs(1) - 1

    @pl.when(pl.program_id(1) < last)
    def _():
        o_ref[...] = _dot(u_ref[...], w_ref[...])

    @pl.when(pl.program_id(1) == last)
    def _():
        o_ref[:, :last_width] = _dot(u_ref[...], w_ref[:, :last_width])


def _inproj(x2d, norm_w, w_in, tm, tn):
    T, D = x2d.shape
    N = w_in.shape[1]
    n_col_blocks = pl.cdiv(N, tn)
    last_width = min(tn, -(-(N - (n_col_blocks - 1) * tn) // LANES) * LANES)
    return pl.pallas_call(
        functools.partial(_inproj_kernel, last_width=last_width),
        out_shape=jax.ShapeDtypeStruct((T, N), F32),
        grid=(T // tm, pl.cdiv(N, tn)),
        in_specs=[
            pl.BlockSpec((tm, D), lambda i, j: (i, 0)),
            pl.BlockSpec((1, D), lambda i, j: (0, 0)),
            pl.BlockSpec((D, tn), lambda i, j: (0, j)),
        ],
        out_specs=pl.BlockSpec((tm, tn), lambda i, j: (i, j)),
        scratch_shapes=[pltpu.VMEM((tm, D), BF16)],
        compiler_params=pltpu.CompilerParams(
            dimension_semantics=("arbitrary", "arbitrary"), vmem_limit_bytes=56 * MIB),
        name="inproj",
    )(x2d, norm_w.reshape(1, D), w_in.astype(BF16))


def _attn_kernel(sink_ref, q_ref, kp_ref, kc_ref, vp_ref, vc_ref, qw_ref, kw_ref, o_ref):
    n = pl.program_id(1)
    L = WINDOW
    d = ATTN_HEAD_DIM

    def head_norm(t, w):
        ms = jnp.mean(t * t, axis=-1, keepdims=True)
        return t * lax.rsqrt(ms + QK_NORM_EPS) * w

    G = ATTN_GROUP
    qi = lax.broadcasted_iota(jnp.int32, (G * L, 2 * L), 0)
    kj = lax.broadcasted_iota(jnp.int32, (G * L, 2 * L), 1)
    dist = (qi & (L - 1)) + L - kj
    valid = (dist >= 0) & (dist < WINDOW) & ((kj >= L) | (n > 0))
    dist_f = dist.astype(F32)
    grp = lax.broadcasted_iota(jnp.int32, (G * L, 1), 0) >> (L.bit_length() - 1)

    def per_row(vals):
        col = jnp.full((G * L, 1), vals[G - 1], F32)
        for g in range(G - 2, -1, -1):
            col = jnp.where(grp == g, vals[g], col)
        return col

    qw = qw_ref[...]
    kw = kw_ref[...]
    kvs = range(ATTN_KV_HEADS)
    heads = lambda kv: [kv * G + g for g in range(G)]
    cols = lambda h: slice(h * d, (h + 1) * d)
    k = [head_norm(jnp.concatenate([kp_ref[:, cols(kv)], kc_ref[:, cols(kv)]], axis=0), kw).astype(BF16)
         for kv in kvs]
    v = [jnp.concatenate([vp_ref[:, cols(kv)], vc_ref[:, cols(kv)]], axis=0).astype(BF16) for kv in kvs]
    q = [head_norm(jnp.concatenate([q_ref[:, cols(h)] for h in heads(kv)], axis=0), qw).astype(BF16)
         for kv in kvs]
    s = [_dot_nt(q[kv], k[kv]) * (d ** -0.5) for kv in kvs]
    slope = [per_row([2.0 ** (-8.0 * (h + 1) / ATTN_HEADS) for h in heads(kv)]) for kv in kvs]
    sink = [per_row([sink_ref[h] for h in heads(kv)]) for kv in kvs]
    s = [jnp.where(valid, s[kv] - slope[kv] * dist_f, -jnp.inf) for kv in kvs]
    m = [jnp.maximum(jnp.max(s[kv], axis=-1, keepdims=True), sink[kv]) for kv in kvs]
    p = [jnp.exp(s[kv] - m[kv]) for kv in kvs]
    denom = [jnp.sum(p[kv], axis=-1, keepdims=True) + jnp.exp(sink[kv] - m[kv]) for kv in kvs]
    probs = [(p[kv] / denom[kv]).astype(BF16) for kv in kvs]
    o = [_dot(probs[kv], v[kv]) for kv in kvs]
    for kv in kvs:
        for g, h in enumerate(heads(kv)):
            o_ref[:, cols(h)] = o[kv][g * L:(g + 1) * L, :].astype(o_ref.dtype)


def _attention(proj, sinks, q_norm_w, k_norm_w, B, S):
    nb = S // WINDOW
    L = WINDOW
    row = lambda b, n: b * nb + n
    prev = lambda b, n: b * nb + jnp.maximum(n - 1, 0)
    return pl.pallas_call(
        _attn_kernel,
        out_shape=jax.ShapeDtypeStruct((B * S, ATTN_Q_W), BF16),
        grid=(B, nb),
        in_specs=[
            pl.BlockSpec(memory_space=pltpu.SMEM),
            pl.BlockSpec((L, ATTN_Q_W), lambda b, n: (row(b, n), 0)),
            pl.BlockSpec((L, ATTN_KV_W), lambda b, n: (prev(b, n), ATTN_K_BLK)),
            pl.BlockSpec((L, ATTN_KV_W), lambda b, n: (row(b, n), ATTN_K_BLK)),
            pl.BlockSpec((L, ATTN_KV_W), lambda b, n: (prev(b, n), ATTN_V_BLK)),
            pl.BlockSpec((L, ATTN_KV_W), lambda b, n: (row(b, n), ATTN_V_BLK)),
            pl.BlockSpec((1, ATTN_HEAD_DIM), lambda b, n: (0, 0)),
            pl.BlockSpec((1, ATTN_HEAD_DIM), lambda b, n: (0, 0)),
        ],
        out_specs=pl.BlockSpec((L, ATTN_Q_W), lambda b, n: (row(b, n), 0)),
        compiler_params=pltpu.CompilerParams(dimension_semantics=("arbitrary", "arbitrary")),
        name="attn",
    )(sinks, proj, proj, proj, proj, proj, q_norm_w.reshape(1, -1), k_norm_w.reshape(1, -1))


CONV_PAD = 8


DN_GROUP = 16

_X3 = ((0, 0), (1, 0), (0, 1))
_EXACT_LHS = ((0, 0), (0, 1), (0, 2))
_EXACT_RHS = ((0, 0), (1, 0), (2, 0))


def _split(x, n):
    parts = []
    for i in range(n):
        p = x.astype(BF16)
        parts.append(p)
        if i + 1 < n:
            x = x - p.astype(F32)
    return parts


def _mm(a_parts, b_parts, terms, form='nn'):
    a_axis = 0 if form == 'tn' else 1
    b_axis = 1 if form == 'nt' else 0
    a = jnp.concatenate([a_parts[i] for i, _ in terms], axis=a_axis)
    b = jnp.concatenate([b_parts[j] for _, j in terms], axis=b_axis)
    return {'nn': _dot, 'nt': _dot_nt, 'tn': _dot_tn}[form](a, b)


def _dn_front(h, alog_ref, dtb_ref, q_ref, k_ref, v_ref, ba_ref, cq_ref, ck_ref, cv_ref,
              xp, qs, ks, vs, gs, bs, us, ws, am, pm, qm, gl):
    S = q_ref.shape[0]
    C = DN_CHUNK
    dk = DN_HEAD_DIM
    piece = min(S, 256)

    def conv_silu(x_ref, cw_ref, dst, l2):
        xp[0:CONV_PAD, :] = jnp.zeros((CONV_PAD, dk), F32)
        xp[CONV_PAD:CONV_PAD + S, :] = x_ref[...]
        for p in range(S // piece):
            r0 = p * piece
            acc = jnp.zeros((piece, dk), F32)
            for i in range(CONV_WIDTH):
                off = CONV_PAD + r0 - (CONV_WIDTH - 1) + i
                acc = acc + xp[off:off + piece, :] * cw_ref[i:i + 1, :]
            y = acc * jax.nn.sigmoid(acc)
            if l2:
                y = y * lax.rsqrt(jnp.sum(y * y, axis=-1, keepdims=True) + L2_EPS)
            dst[r0:r0 + piece, :] = y

    conv_silu(q_ref, cq_ref, qs, True)
    conv_silu(k_ref, ck_ref, ks, True)
    conv_silu(v_ref, cv_ref, vs, False)

    neg_a = -jnp.exp(jnp.full((1, dk), alog_ref[h], F32))
    dtb = dtb_ref[h]
    for p in range(S // piece):
        r0 = p * piece
        ba = ba_ref[r0:r0 + piece, :]
        lane = lax.broadcasted_iota(jnp.int32, ba.shape, 1)
        b_col = jnp.sum(jnp.where(lane == h, ba, 0.0), axis=-1, keepdims=True)
        a_col = jnp.sum(jnp.where(lane == DN_HEADS + h, ba, 0.0), axis=-1, keepdims=True)
        bs[r0:r0 + piece, :] = jnp.broadcast_to(jax.nn.sigmoid(b_col), (piece, dk))
        gs[r0:r0 + piece, :] = neg_a * jnp.broadcast_to(jax.nn.softplus(a_col + dtb), (piece, dk))

    ri = lax.broadcasted_iota(jnp.int32, (C, C), 0)
    ci = lax.broadcasted_iota(jnp.int32, (C, C), 1)
    incl = ri >= ci
    strict = ri > ci
    tri_ones = [jnp.where(incl, 1.0, 0.0).astype(BF16)]
    upper_ones = [jnp.where(ri <= ci, 1.0, 0.0).astype(BF16)]
    n_chunks = S // C
    group = DN_GROUP if n_chunks % DN_GROUP == 0 else 1

    def intra_load(c):
        rows = pl.ds(pl.multiple_of(c * C, C), C)
        return qs[rows, :], ks[rows, :], vs[rows, :], gs[rows, :], bs[rows, :]

    def intra_compute(loaded):
        each = lambda f, *ls: [f(*a) for a in zip(*ls)]
        q, k, v, g, beta = (list(t) for t in zip(*loaded))
        q = each(lambda t: t * (dk ** -0.5), q)
        g3 = each(lambda t: _split(t, 3), g)
        gc = each(lambda p: _mm(tri_ones, p, _EXACT_LHS), g3)
        gc_row = each(lambda p: _mm([t[:, :C] for t in p], upper_ones, _EXACT_RHS, 'tn'), g3)
        decay = each(lambda a, b: jnp.exp(jnp.where(incl, a[:, :C] - b, -jnp.inf)), gc, gc_row)
        kb = each(jnp.multiply, k, beta)
        k2 = each(lambda t: _split(t, 2), k)
        kk = each(lambda a, b: _mm(_split(a, 2), b, _X3, 'nt'), kb, k2)
        qk = each(lambda a, b: _mm(_split(a, 2), b, _X3, 'nt'), q, k2)
        m = each(lambda a, d: -jnp.where(strict, a * d, 0.0), kk, decay)
        x = each(lambda vv, bb, kbb, gg: jnp.concatenate([vv * bb, kbb * jnp.exp(gg)], axis=1),
                 v, beta, kb, gc)
        n_fac = int(math.log2(C))
        for it in range(n_fac):
            m2 = each(lambda t: _split(t, 2), m)
            x = each(lambda xx, mm: xx + _mm(mm, _split(xx, 2), _X3), x, m2)
            if it + 1 < n_fac:
                m = each(lambda mm: _mm(mm, mm, _X3), m2)
        kd2 = each(lambda kk_, gg: _split(kk_ * jnp.exp(gg[C - 1:C, :] - gg), 2), k, gc)
        pq = each(lambda a, xx: _mm(a, _split(xx, 2), _X3, 'tn'), kd2, x)
        a_intra = each(lambda a, d: jnp.where(incl, a * d, 0.0), qk, decay)
        qg = each(lambda a, gg: a * jnp.exp(gg), q, gc)
        e_last = each(lambda gg: jnp.broadcast_to(jnp.exp(gg[C - 1:C, :]), (8, dk)), gc)
        return list(zip(x, qg, a_intra, pq, e_last))

    def intra_store(c, x, qg, a_intra, pq, e_last):
        rows = pl.ds(pl.multiple_of(c * C, C), C)
        prow = pl.ds(pl.multiple_of(c * dk, dk), dk)
        us[rows, :] = x[:, :dk]
        ws[rows, :] = x[:, dk:]
        qs[rows, :] = qg
        am[rows, :] = a_intra
        qm[prow, :] = pq[:, :dk]
        pm[prow, :] = pq[:, dk:]
        gl[pl.ds(pl.multiple_of(c * 8, 8), 8), :] = e_last

    def intra_group(i, _):
        loaded = [intra_load(i * group + t) for t in range(group)]
        outs = intra_compute(loaded)
        for t in range(group):
            intra_store(i * group + t, *outs[t])
        return 0

    lax.fori_loop(0, n_chunks // group, intra_group, 0)


def _dn_scan(heads):
    dk = DN_HEAD_DIM
    n_chunks = heads[0][0].shape[0] // dk

    def scan(c, states):
        prow = pl.ds(pl.multiple_of(c * dk, dk), dk)
        ps = [_mm(_split(pm[prow, :], 2), _split(state, 2), _X3) for (pm, _, _), state in zip(heads, states)]
        for (pm, _, _), state in zip(heads, states):
            pm[prow, :] = state
        e_last = [gl[pl.ds(pl.multiple_of(c * 8, 8), 1), :] for _, _, gl in heads]
        return tuple(state * e - p + qm[prow, :]
                     for (_, qm, _), state, e, p in zip(heads, states, e_last, ps))

    lax.fori_loop(0, n_chunks, scan, tuple(jnp.zeros((dk, dk), F32) for _ in heads), unroll=2)


def _dn_back(z_ref, nw_ref, o_ref, qs, us, ws, am, pm):
    S = z_ref.shape[0]
    C = DN_CHUNK
    dk = DN_HEAD_DIM
    n_chunks = S // C
    group = DN_GROUP if n_chunks % DN_GROUP == 0 else 1
    nw = nw_ref[...]

    def out_group(i, _):
        cs = [i * group + t for t in range(group)]
        rows = [pl.ds(pl.multiple_of(c * C, C), C) for c in cs]
        each = lambda f, *ls: [f(*a) for a in zip(*ls)]
        state2 = each(lambda c: _split(pm[pl.ds(pl.multiple_of(c * dk, dk), dk), :], 2), cs)
        lhs2 = each(lambda r: _split(jnp.concatenate([ws[r, :], qs[r, :]], axis=0), 2), rows)
        prod = each(lambda a, b: _mm(a, b, _X3), lhs2, state2)
        v_new = each(lambda r, p: us[r, :] - p[:C], rows, prod)
        av = each(lambda r, vn: _mm(_split(am[r, :], 2), _split(vn, 2), _X3), rows, v_new)
        o = each(lambda p, a: p[C:] + a, prod, av)
        o = each(lambda t: t * lax.rsqrt(jnp.mean(t * t, axis=-1, keepdims=True) + DN_NORM_EPS) * nw, o)
        for r, t in zip(rows, o):
            z = z_ref[r, :]
            o_ref[r, :] = (t * (z * jax.nn.sigmoid(z))).astype(o_ref.dtype)
        return 0

    lax.fori_loop(0, n_chunks // group, out_group, 0)


DN_HEADS_PER_STEP = 2


def _dn_kernel(alog_ref, dtb_ref, q_ref, k_ref, v_ref, z_ref, ba_ref, cq_ref, ck_ref, cv_ref,
               nw_ref, o_ref, xp, *scratch):
    dk = DN_HEAD_DIM
    n_heads = o_ref.shape[1] // dk
    lanes = lambda ref, hh: ref.at[pl.ds(0, ref.shape[0]), pl.ds(hh * dk, dk)]
    views = [tuple(r.at[hh] for r in scratch) for hh in range(n_heads)]
    for hh in range(n_heads):
        _dn_front(pl.program_id(1) * n_heads + hh, alog_ref, dtb_ref, lanes(q_ref, hh), lanes(k_ref, hh),
                  lanes(v_ref, hh), ba_ref, lanes(cq_ref, hh), lanes(ck_ref, hh), lanes(cv_ref, hh),
                  xp, *views[hh])
    _dn_scan([(pm, qm, gl) for (qs, ks, vs, gs, bs, us, ws, am, pm, qm, gl) in views])
    for hh, (qs, ks, vs, gs, bs, us, ws, am, pm, qm, gl) in enumerate(views):
        _dn_back(lanes(z_ref, hh), nw_ref, lanes(o_ref, hh), qs, us, ws, am, pm)


def _deltanet(proj, conv_w, a_log, dt_bias, dn_norm_w, B, S):
    dk = DN_HEAD_DIM
    hp = DN_HEADS_PER_STEP
    seq = lambda blk: pl.BlockSpec((S, hp * dk), lambda b, h: (b, blk // hp + h))
    cw = lambda blk: pl.BlockSpec((CONV_WIDTH, hp * dk), lambda b, h: (0, blk // hp + h))
    assert all(blk % hp == 0 for blk in (DN_Q_BLK, DN_K_BLK, DN_V_BLK, DN_Z_BLK, DN_HEADS))
    return pl.pallas_call(
        _dn_kernel,
        out_shape=jax.ShapeDtypeStruct((B * S, DN_W), BF16),
        grid=(B, DN_HEADS // hp),
        in_specs=[
            pl.BlockSpec(memory_space=pltpu.SMEM),
            pl.BlockSpec(memory_space=pltpu.SMEM),
            seq(DN_Q_BLK), seq(DN_K_BLK), seq(DN_V_BLK), seq(DN_Z_BLK),
            pl.BlockSpec((S, LANES), lambda b, h: (b, DN_BA_BLK)),
            cw(0), cw(DN_HEADS), cw(2 * DN_HEADS),
            pl.BlockSpec((1, dk), lambda b, h: (0, 0)),
        ],
        out_specs=pl.BlockSpec((S, hp * dk), lambda b, h: (b, h)),
        scratch_shapes=[pltpu.VMEM((S + CONV_PAD, dk), F32)] + [pltpu.VMEM((hp, S, dk), F32)] * 7
        + [pltpu.VMEM((hp, S, DN_CHUNK), F32)] + [pltpu.VMEM((hp, S // DN_CHUNK * dk, dk), F32)] * 2
        + [pltpu.VMEM((hp, S // DN_CHUNK * 8, dk), F32)],
        compiler_params=pltpu.CompilerParams(
            dimension_semantics=("arbitrary", "arbitrary"), vmem_limit_bytes=58 * MIB),
        name="deltanet",
    )(a_log, dt_bias, proj, proj, proj, proj, proj, conv_w, conv_w, conv_w,
      dn_norm_w.reshape(1, dk))


def _pack_bf16_pairs(x):
    c = x.shape[1] // 2
    bits = lambda t: lax.bitcast_convert_type(t.astype(BF16).astype(F32), jnp.uint32)
    return (bits(x[:, :c]) >> 16) | (bits(x[:, c:]) & jnp.uint32(0xFFFF0000))


def _unpack_bf16_pairs(p):
    lo = lax.bitcast_convert_type(p << 16, F32).astype(BF16)
    hi = lax.bitcast_convert_type(p & jnp.uint32(0xFFFF0000), F32).astype(BF16)
    return lo, hi


def _outproj_kernel(a_ref, d_ref, x_ref, wa_ref, wd_ref, nw_ref, wrh_ref, wrl_ref, br_ref,
                    h_ref, u_ref, ti_ref, tg_ref):
    h = x_ref[...] + _dot(a_ref[...], wa_ref[...]) + _dot(d_ref[...], wd_ref[...])
    h_ref[...] = h
    ms = jnp.mean(h * h, axis=-1, keepdims=True)
    u = h * lax.rsqrt(ms + NORM_EPS) * nw_ref[...]
    u_ref[...] = _pack_bf16_pairs(u)
    u_hi, u_lo = _split(u, 2)
    logits = (_dot(u_hi, wrh_ref[...]) + _dot(u_lo, wrh_ref[...]) + _dot(u_hi, wrl_ref[...])
              + br_ref[...])
    lane = lax.broadcasted_iota(jnp.int32, logits.shape, 1)
    lane_f = lane.astype(F32)
    vals = jnp.where(lane < N_EXPERTS, logits, -jnp.inf)
    top_v, top_i = [], []
    for _ in range(TOP_K):
        m = jnp.max(vals, axis=-1, keepdims=True)
        idx = jnp.min(jnp.where(vals == m, lane_f, float(LANES)), axis=-1, keepdims=True)
        top_v.append(m)
        top_i.append(idx)
        vals = jnp.where(lane_f == idx, -jnp.inf, vals)
    e = [jnp.exp(v - top_v[0]) for v in top_v]
    denom = e[0] + e[1] + e[2] + e[3]
    ti = jnp.zeros(logits.shape, F32)
    tg = jnp.zeros(logits.shape, F32)
    for k in range(TOP_K):
        ti = jnp.where(lane == k, top_i[k], ti)
        tg = jnp.where(lane == k, e[k] / denom, tg)
    ti_ref[...] = ti.astype(jnp.int32)
    tg_ref[...] = tg


def _outproj_router(attn_o, dn_o, x2d, w_out, ffn_norm_w, w_router, b_router, tm):
    T, D = x2d.shape
    wa = w_out[:ATTN_Q_W].astype(BF16)
    wd = w_out[ATTN_Q_W:].astype(BF16)
    wr = jnp.zeros((D, LANES), F32).at[:, :N_EXPERTS].set(w_router)
    wr_hi = wr.astype(BF16)
    wr_lo = (wr - wr_hi.astype(F32)).astype(BF16)
    br = jnp.zeros((1, LANES), F32).at[0, :N_EXPERTS].set(b_router)
    const = lambda shape: pl.BlockSpec(shape, lambda i: (0, 0))
    tile = lambda w: pl.BlockSpec((tm, w), lambda i: (i, 0))
    return pl.pallas_call(
        _outproj_kernel,
        out_shape=(jax.ShapeDtypeStruct((T, D), F32), jax.ShapeDtypeStruct((T, D // 2), jnp.uint32),
                   jax.ShapeDtypeStruct((T, LANES), jnp.int32),
                   jax.ShapeDtypeStruct((T, LANES), F32)),
        grid=(T // tm,),
        in_specs=[tile(ATTN_Q_W), tile(DN_W), tile(D), const((ATTN_Q_W, D)), const((DN_W, D)),
                  const((1, D)), const((D, LANES)), const((D, LANES)), const((1, LANES))],
        out_specs=(tile(D), tile(D // 2), tile(LANES), tile(LANES)),
        compiler_params=pltpu.CompilerParams(
            dimension_semantics=("arbitrary",), vmem_limit_bytes=56 * MIB),
        name="outproj",
    )(attn_o, dn_o, x2d, wa, wd, ffn_norm_w.reshape(1, D), wr_hi, wr_lo, br)


def _experts_kernel(ie_ref, ib_ref, in_ref, used_ref, x_ref, wg_ref, wl_ref, wd_ref,
                    bg_ref, bl_ref, bd_ref, y_ref, xraw, ybuf, wg_s, wl_s, wd_s, sem_x, sem_y):
    w = pl.program_id(0)
    j = pl.program_id(1)
    nw = pl.num_programs(0)
    nj = pl.num_programs(1)
    R = MOE_ROWS
    nb = in_ref[w]
    row0 = ib_ref[w] * R
    slot = lax.rem(w, 2)
    w_next = jnp.minimum(w + 1, nw - 1)
    nb_next = jnp.where(w + 1 < nw, in_ref[w_next], 0)
    row0_next = ib_ref[w_next] * R

    def x_copy(base, s, r):
        return pltpu.make_async_copy(x_ref.at[pl.ds(base + r * R, R), :],
                                     xraw.at[s, pl.ds(r * R, R), :], sem_x.at[s])

    def y_copy(r):
        return pltpu.make_async_copy(ybuf.at[pl.ds(r * R, R), :],
                                     y_ref.at[pl.ds(row0 + r * R, R), :], sem_y)

    def for_range(n, fn):
        def body(r, _):
            fn(r)
            return 0
        lax.fori_loop(0, n, body, 0)

    def rows_of(r):
        return pl.ds(pl.multiple_of(r * R, R), R)

    @pl.when((j == 0) & (w == 0))
    def _():
        for_range(nb, lambda r: x_copy(row0, slot, r).start())

    @pl.when(j == 0)
    def _():
        bd = jnp.broadcast_to(bd_ref[...], (R, ybuf.shape[1]))

        def init(r):
            ybuf[rows_of(r), :] = bd

        for_range(nb, init)
        for_range(nb, lambda r: x_copy(row0, slot, r).wait())
        for_range(nb_next, lambda r: x_copy(row0_next, 1 - slot, r).start())

    @pl.when(nb > 0)
    def _():
        wg_s[...] = wg_ref[...].astype(BF16)
        wl_s[...] = wl_ref[...].astype(BF16)
        wd_s[...] = wd_ref[...].astype(BF16)

    bg = bg_ref[...]
    bl = bl_ref[...]

    def blocks(rs, write_back):
        rows = [rows_of(r) for r in rs]
        xs = [jnp.concatenate(_unpack_bf16_pairs(xraw[slot, rw, :]), axis=1) for rw in rows]
        hg = [jnp.minimum(_dot(x, wg_s[...]) + bg, SWIGLU_LIMIT) for x in xs]
        hl = [jnp.clip(_dot(x, wl_s[...]) + bl, -SWIGLU_LIMIT, SWIGLU_LIMIT) for x in xs]
        acts = [(a * jax.nn.sigmoid(SWIGLU_ALPHA * a) * (b + 1.0)).astype(BF16) for a, b in zip(hg, hl)]
        for rw, act in zip(rows, acts):
            ybuf[rw, :] += _dot(act, wd_s[...])
        if write_back:
            for r in rs:
                y_copy(r).start()

    def all_blocks(write_back):
        def pair(i, _):
            blocks([2 * i, 2 * i + 1], write_back)
            return 0

        lax.fori_loop(0, nb // 2, pair, 0)

        @pl.when(nb % 2 == 1)
        def _():
            blocks([nb - 1], write_back)

    @pl.when(j < nj - 1)
    def _():
        all_blocks(False)

    @pl.when(j == nj - 1)
    def _():
        all_blocks(True)
        for_range(nb, lambda r: y_copy(r).wait())

    @pl.when((j == nj - 1) & (w == pl.num_programs(0) - 1))
    def _():
        ybuf[0:R, :] = jnp.zeros((R, ybuf.shape[1]), F32)

        def tail_copy(r):
            return pltpu.make_async_copy(ybuf.at[pl.ds(0, R), :], y_ref.at[pl.ds(r * R, R), :], sem_y)

        def tail(fn):
            def body(r, _):
                fn(r)
                return 0
            lax.fori_loop(used_ref[0], y_ref.shape[0] // R, body, 0)

        tail(lambda r: tail_copy(r).start())
        tail(lambda r: tail_copy(r).wait())


def _experts(x_sorted, item_expert, item_blk0, item_nblk, n_blocks_used,
             w_gate_up, b_gate_up, w_down, b_down):
    P = x_sorted.shape[0]
    D = x_sorted.shape[1] * 2
    E, _, F2 = w_gate_up.shape
    F = F2 // 2
    tf = MOE_F_TILE
    nj = F // tf
    n_items = item_expert.shape[0]
    rows = MOE_ITEM_BLOCKS * MOE_ROWS
    b_gu = b_gate_up.reshape(E, 1, F2)
    b_d = b_down.reshape(E, 1, D)
    jt = lambda j, n: jnp.where(n > 0, j, nj - 1)
    return pl.pallas_call(
        _experts_kernel,
        out_shape=jax.ShapeDtypeStruct((P, D), F32),
        grid_spec=pltpu.PrefetchScalarGridSpec(
            num_scalar_prefetch=4,
            grid=(n_items, nj),
            in_specs=[
                pl.BlockSpec(memory_space=pl.ANY),
                pl.BlockSpec((None, D, tf), lambda w, j, ie, ib, nb, *_: (ie[w], 0, jt(j, nb[w]))),
                pl.BlockSpec((None, D, tf), lambda w, j, ie, ib, nb, *_: (ie[w], 0, nj + jt(j, nb[w]))),
                pl.BlockSpec((None, tf, D), lambda w, j, ie, ib, nb, *_: (ie[w], jt(j, nb[w]), 0)),
                pl.BlockSpec((None, 1, tf), lambda w, j, ie, ib, nb, *_: (ie[w], 0, jt(j, nb[w]))),
                pl.BlockSpec((None, 1, tf), lambda w, j, ie, ib, nb, *_: (ie[w], 0, nj + jt(j, nb[w]))),
                pl.BlockSpec((None, 1, D), lambda w, j, ie, *_: (ie[w], 0, 0)),
            ],
            out_specs=pl.BlockSpec(memory_space=pl.ANY),
            scratch_shapes=[
                pltpu.VMEM((2, rows, D // 2), jnp.uint32), pltpu.VMEM((rows, D), F32),
                pltpu.VMEM((D, tf), BF16), pltpu.VMEM((D, tf), BF16), pltpu.VMEM((tf, D), BF16),
                pltpu.SemaphoreType.DMA((2,)), pltpu.SemaphoreType.DMA,
            ],
        ),
        compiler_params=pltpu.CompilerParams(
            dimension_semantics=("arbitrary", "arbitrary"), vmem_limit_bytes=60 * MIB),
        name="experts",
    )(item_expert, item_blk0, item_nblk, n_blocks_used, x_sorted,
      w_gate_up, w_gate_up, w_down, b_gu, b_gu, b_d)


def _combine_kernel(dest_ref, h_ref, tg_ref, y_ref, o_ref, buf, sem):
    i = pl.program_id(0)
    n = pl.num_programs(0)
    tm = h_ref.shape[0]

    def issue_tile(step):
        slot = lax.rem(step, 2)

        def issue(r, _):
            for t in range(ROWS_PER_ISSUE):
                for k in range(TOP_K):
                    row = dest_ref[(step * tm + ROWS_PER_ISSUE * r + t) * TOP_K + k]
                    pltpu.make_async_copy(y_ref.at[pl.ds(row, 1), :],
                                          buf.at[slot, k, pl.ds(ROWS_PER_ISSUE * r + t, 1), :],
                                          sem.at[slot]).start()
            return 0

        lax.fori_loop(0, tm // ROWS_PER_ISSUE, issue, 0)

    @pl.when(i == 0)
    def _():
        issue_tile(0)

    @pl.when(i + 1 < n)
    def _():
        issue_tile(i + 1)

    slot = lax.rem(i, 2)
    for k in range(TOP_K):
        pltpu.make_async_copy(y_ref.at[pl.ds(0, tm), :], buf.at[slot, k], sem.at[slot]).wait()
    tg = tg_ref[...]
    acc = h_ref[...]
    for k in range(TOP_K):
        acc = acc + tg[:, k:k + 1] * buf[slot, k]
    o_ref[...] = acc


def _combine(h, top_g, y_sorted, dest, tm):
    T, D = h.shape
    return pl.pallas_call(
        _combine_kernel,
        out_shape=jax.ShapeDtypeStruct((T, D), F32),
        grid_spec=pltpu.PrefetchScalarGridSpec(
            num_scalar_prefetch=1,
            grid=(T // tm,),
            in_specs=[
                pl.BlockSpec((tm, D), lambda i, d: (i, 0)),
                pl.BlockSpec((tm, LANES), lambda i, d: (i, 0)),
                pl.BlockSpec(memory_space=pl.ANY),
            ],
            out_specs=pl.BlockSpec((tm, D), lambda i, d: (i, 0)),
            scratch_shapes=[pltpu.VMEM((2, TOP_K, tm, D), F32), pltpu.SemaphoreType.DMA((2,))],
        ),
        compiler_params=pltpu.CompilerParams(dimension_semantics=("arbitrary",)),
        name="combine",
    )(dest, h, top_g, y_sorted)


def _dispatch_kernel(dest_ref, row_end_ref, blk_end_ref, u_ref, x_ref, buf, zbuf, sem_l, sem_s, sem_z):
    i = pl.program_id(0)
    n = pl.num_programs(0)
    tm = buf.shape[1]
    R = MOE_ROWS
    n_blocks = x_ref.shape[0] // R

    def load(step):
        s = lax.rem(step, 2)
        return pltpu.make_async_copy(u_ref.at[pl.ds(step * tm, tm), :], buf.at[s], sem_l.at[s])

    def zero_copy(blk):
        return pltpu.make_async_copy(zbuf, x_ref.at[pl.ds(blk * R, R), :], sem_z)

    def wait_scatter(step):
        s = lax.rem(step, 2)
        for _ in range(TOP_K):
            pltpu.make_async_copy(u_ref.at[pl.ds(0, tm), :], buf.at[s], sem_s.at[s]).wait()

    @pl.when(i == 0)
    def _():
        zbuf[...] = jnp.zeros(zbuf.shape, zbuf.dtype)

        def partial_blocks(fn):
            def body(e, _):
                @pl.when(lax.rem(row_end_ref[e], R) != 0)
                def _():
                    fn(blk_end_ref[e] - 1)
                return 0
            lax.fori_loop(0, N_EXPERTS, body, 0)

        def unused_blocks(fn):
            def body(b, _):
                fn(b)
                return 0
            lax.fori_loop(blk_end_ref[N_EXPERTS - 1], n_blocks, body, 0)

        partial_blocks(lambda b: zero_copy(b).start())
        unused_blocks(lambda b: zero_copy(b).start())
        partial_blocks(lambda b: zero_copy(b).wait())
        unused_blocks(lambda b: zero_copy(b).wait())
        load(0).start()

    @pl.when(i > 0)
    def _():
        wait_scatter(i - 1)

    @pl.when(i + 1 < n)
    def _():
        load(i + 1).start()

    load(i).wait()
    slot = lax.rem(i, 2)

    def issue(r, _):
        for t in range(ROWS_PER_ISSUE):
            for k in range(TOP_K):
                row = dest_ref[(i * tm + ROWS_PER_ISSUE * r + t) * TOP_K + k]
                pltpu.make_async_copy(buf.at[slot, pl.ds(ROWS_PER_ISSUE * r + t, 1), :],
                                      x_ref.at[pl.ds(row, 1), :], sem_s.at[slot]).start()
        return 0

    lax.fori_loop(0, tm // ROWS_PER_ISSUE, issue, 0)

    @pl.when(i == n - 1)
    def _():
        wait_scatter(i)


def _dispatch(u_packed, dest, row_end, blk_end, P, tm):
    T, W = u_packed.shape
    return pl.pallas_call(
        _dispatch_kernel,
        out_shape=jax.ShapeDtypeStruct((P, W), u_packed.dtype),
        grid_spec=pltpu.PrefetchScalarGridSpec(
            num_scalar_prefetch=3,
            grid=(T // tm,),
            in_specs=[pl.BlockSpec(memory_space=pl.ANY)],
            out_specs=pl.BlockSpec(memory_space=pl.ANY),
            scratch_shapes=[pltpu.VMEM((2, tm, W), u_packed.dtype), pltpu.VMEM((MOE_ROWS, W), u_packed.dtype),
                            pltpu.SemaphoreType.DMA((2,)), pltpu.SemaphoreType.DMA((2,)),
                            pltpu.SemaphoreType.DMA],
        ),
        compiler_params=pltpu.CompilerParams(dimension_semantics=("arbitrary",)),
        name="dispatch",
    )(dest, row_end, blk_end, u_packed)


def _routing(top_i, T):
    R = MOE_ROWS
    n_assign = T * TOP_K
    e_flat = top_i[:, :TOP_K].reshape(n_assign)
    onehot = (e_flat[:, None] == jnp.arange(N_EXPERTS, dtype=jnp.int32)[None, :]).astype(jnp.int32)
    csum = jnp.cumsum(onehot, axis=0)
    rank = jnp.take_along_axis(csum, e_flat[:, None], axis=1)[:, 0] - 1
    counts = csum[-1]
    nblk = (counts + R - 1) // R
    blk_end = jnp.cumsum(nblk)
    blk_start = blk_end - nblk
    dest = blk_start[e_flat] * R + rank
    max_blocks = n_assign // R + N_EXPERTS
    P = max_blocks * R
    dest = dest.astype(jnp.int32)
    row_end = (blk_start * R + counts).astype(jnp.int32)
    nb_item = MOE_ITEM_BLOCKS
    n_items_e = (nblk + nb_item - 1) // nb_item
    item_end = jnp.cumsum(n_items_e)
    item_start = item_end - n_items_e
    max_items = N_EXPERTS + -(-max_blocks // nb_item)
    w = jnp.arange(max_items, dtype=jnp.int32)
    ie = jnp.minimum(jnp.searchsorted(item_end, w, side='right'), N_EXPERTS - 1).astype(jnp.int32)
    local = w - item_start[ie]
    live = w < item_end[-1]
    ib = blk_start[ie] + local * nb_item
    inb = jnp.clip(nblk[ie] - local * nb_item, 0, nb_item)
    last_e = ie[jnp.maximum(item_end[-1] - 1, 0)]
    ie = jnp.where(live, ie, last_e).astype(jnp.int32)
    ib = jnp.where(live, ib, 0).astype(jnp.int32)
    inb = jnp.where(live, inb, 0).astype(jnp.int32)
    return dest, row_end, blk_end.astype(jnp.int32), P, ie, ib, inb


def _layer(x, attn_norm_w, w_in, q_norm_w, k_norm_w, attn_sinks, conv_w, a_log, dt_bias,
           dn_norm_w, w_out, ffn_norm_w, w_router, b_router, w_gate_up, b_gate_up, w_down, b_down):
    B, S, D = x.shape
    T = B * S
    x2d = x.reshape(T, D)
    proj = _inproj(x2d, attn_norm_w, w_in, tm=min(T, 1024), tn=1024)
    attn_o = _attention(proj, attn_sinks, q_norm_w, k_norm_w, B, S)
    dn_o = _deltanet(proj, conv_w, a_log, dt_bias, dn_norm_w, B, S)
    h, u, top_i, top_g = _outproj_router(attn_o, dn_o, x2d, w_out, ffn_norm_w, w_router, b_router,
                                         tm=min(T, 512))
    dest, row_end, blk_end, P, ie, ib, inb = _routing(top_i, T)
    x_sorted = _dispatch(u, dest, row_end, blk_end, P, tm=min(T, 256))
    y_sorted = _experts(x_sorted, ie, ib, inb, blk_end[-1:], w_gate_up, b_gate_up, w_down, b_down)
    out = _combine(h, top_g, y_sorted, dest, tm=min(T, 128))
    return out.reshape(B, S, D)


def kernel(x, attn_norm_w, w_in, q_norm_w, k_norm_w, attn_sinks, conv_w, a_log, dt_bias, dn_norm_w,
           w_out, ffn_norm_w, w_router, b_router, w_gate_up, b_gate_up, w_down, b_down):
    h = x
    for l in range(attn_norm_w.shape[0]):
        h = _layer(h, attn_norm_w[l], w_in[l], q_norm_w[l], k_norm_w[l], attn_sinks[l], conv_w[l],
                   a_log[l], dt_bias[l], dn_norm_w[l], w_out[l], ffn_norm_w[l], w_router[l],
                   b_router[l], w_gate_up[l], b_gate_up[l], w_down[l], b_down[l])
    return h
```

```python
import functools
import math

import jax
import jax.numpy as jnp
from jax import lax
from jax.experimental import pallas as pl
from jax.experimental.pallas import tpu as pltpu

F32 = jnp.float32
BF16 = jnp.bfloat16

D_MODEL = 2048
ATTN_HEAD_DIM = 64
ATTN_HEADS = 16
ATTN_KV_HEADS = 4
ATTN_GROUP = ATTN_HEADS // ATTN_KV_HEADS
WINDOW = 128
DN_HEAD_DIM = 128
DN_HEADS = 8
DN_CHUNK = 64
CONV_WIDTH = 4
ATTN_Q_W = ATTN_HEADS * ATTN_HEAD_DIM
ATTN_KV_W = ATTN_KV_HEADS * ATTN_HEAD_DIM
DN_W = DN_HEADS * DN_HEAD_DIM
IN_WIDTH = ATTN_Q_W + 2 * ATTN_KV_W + 4 * DN_W + 2 * DN_HEADS
N_EXPERTS = 32
TOP_K = 4
D_EXPERT = 2048
SWIGLU_ALPHA = 1.702
SWIGLU_LIMIT = 7.0
NORM_EPS = 1e-5
QK_NORM_EPS = 1e-6
DN_NORM_EPS = 1e-6
L2_EPS = 1e-6

LANES = 128
MIB = 1024 * 1024
HIGHEST = lax.Precision.HIGHEST

ATTN_K_BLK = ATTN_Q_W // ATTN_KV_W
ATTN_V_BLK = ATTN_K_BLK + 1
DN_Q_BLK = (ATTN_Q_W + 2 * ATTN_KV_W) // DN_HEAD_DIM
DN_K_BLK = DN_Q_BLK + DN_HEADS
DN_V_BLK = DN_K_BLK + DN_HEADS
DN_Z_BLK = DN_V_BLK + DN_HEADS
DN_BA_BLK = DN_Z_BLK + DN_HEADS

MOE_ROWS = 256
MOE_ITEM_BLOCKS = 9
MOE_F_TILE = 256
MOE_LOCKSTEP_BLOCKS = 4
ROWS_PER_ISSUE = 4


def _dot(a, b, precision=None):
    return jnp.dot(a, b, preferred_element_type=F32, precision=precision)


def _dot_nt(a, b, precision=None):
    return lax.dot_general(a, b, (((1,), (1,)), ((), ())), preferred_element_type=F32,
                           precision=precision)


def _dot_tn(a, b, precision=None):
    return lax.dot_general(a, b, (((0,), (0,)), ((), ())), preferred_element_type=F32,
                           precision=precision)


def _inproj_kernel(x_ref, nw_ref, w_ref, o_ref, u_ref, *, last_width):
    @pl.when(pl.program_id(1) == 0)
    def _():
        x = x_ref[...]
        ms = jnp.mean(x * x, axis=-1, keepdims=True)
        u_ref[...] = (x * lax.rsqrt(ms + NORM_EPS) * nw_ref[...]).astype(BF16)

    last = pl.num_programs(1) - 1

    @pl.when(pl.program_id(1) < last)
    def _():
        o_ref[...] = _dot(u_ref[...], w_ref[...])

    @pl.when(pl.program_id(1) == last)
    def _():
        o_ref[:, :last_width] = _dot(u_ref[...], w_ref[:, :last_width])


def _inproj(x2d, norm_w, w_in, tm, tn):
    T, D = x2d.shape
    N = w_in.shape[1]
    n_col_blocks = pl.cdiv(N, tn)
    last_width = min(tn, -(-(N - (n_col_blocks - 1) * tn) // LANES) * LANES)
    return pl.pallas_call(
        functools.partial(_inproj_kernel, last_width=last_width),
        out_shape=jax.ShapeDtypeStruct((T, N), F32),
        grid=(T // tm, pl.cdiv(N, tn)),
        in_specs=[
            pl.BlockSpec((tm, D), lambda i, j: (i, 0)),
            pl.BlockSpec((1, D), lambda i, j: (0, 0)),
            pl.BlockSpec((D, tn), lambda i, j: (0, j)),
        ],
        out_specs=pl.BlockSpec((tm, tn), lambda i, j: (i, j)),
        scratch_shapes=[pltpu.VMEM((tm, D), BF16)],
        compiler_params=pltpu.CompilerParams(
            dimension_semantics=("arbitrary", "arbitrary"), vmem_limit_bytes=56 * MIB),
        name="inproj",
    )(x2d, norm_w.reshape(1, D), w_in.astype(BF16))


def _attn_kernel(sink_ref, q_ref, kp_ref, kc_ref, vp_ref, vc_ref, qw_ref, kw_ref, o_ref):
    n = pl.program_id(1)
    L = WINDOW
    d = ATTN_HEAD_DIM

    def head_norm(t, w):
        ms = jnp.mean(t * t, axis=-1, keepdims=True)
        return t * lax.rsqrt(ms + QK_NORM_EPS) * w

    G = ATTN_GROUP
    qi = lax.broadcasted_iota(jnp.int32, (G * L, 2 * L), 0)
    kj = lax.broadcasted_iota(jnp.int32, (G * L, 2 * L), 1)
    dist = (qi & (L - 1)) + L - kj
    valid = (dist >= 0) & (dist < WINDOW) & ((kj >= L) | (n > 0))
    dist_f = dist.astype(F32)
    grp = lax.broadcasted_iota(jnp.int32, (G * L, 1), 0) >> (L.bit_length() - 1)

    def per_row(vals):
        col = jnp.full((G * L, 1), vals[G - 1], F32)
        for g in range(G - 2, -1, -1):
            col = jnp.where(grp == g, vals[g], col)
        return col

    qw = qw_ref[...]
    kw = kw_ref[...]
    kvs = range(ATTN_KV_HEADS)
    heads = lambda kv: [kv * G + g for g in range(G)]
    cols = lambda h: slice(h * d, (h + 1) * d)
    k = [head_norm(jnp.concatenate([kp_ref[:, cols(kv)], kc_ref[:, cols(kv)]], axis=0), kw).astype(BF16)
         for kv in kvs]
    v = [jnp.concatenate([vp_ref[:, cols(kv)], vc_ref[:, cols(kv)]], axis=0).astype(BF16) for kv in kvs]
    q = [head_norm(jnp.concatenate([q_ref[:, cols(h)] for h in heads(kv)], axis=0), qw).astype(BF16)
         for kv in kvs]
    s = [_dot_nt(q[kv], k[kv]) * (d ** -0.5) for kv in kvs]
    slope = [per_row([2.0 ** (-8.0 * (h + 1) / ATTN_HEADS) for h in heads(kv)]) for kv in kvs]
    sink = [per_row([sink_ref[h] for h in heads(kv)]) for kv in kvs]
    s = [jnp.where(valid, s[kv] - slope[kv] * dist_f, -jnp.inf) for kv in kvs]
    m = [jnp.maximum(jnp.max(s[kv], axis=-1, keepdims=True), sink[kv]) for kv in kvs]
    p = [jnp.exp(s[kv] - m[kv]) for kv in kvs]
    denom = [jnp.sum(p[kv], axis=-1, keepdims=True) + jnp.exp(sink[kv] - m[kv]) for kv in kvs]
    probs = [(p[kv] / denom[kv]).astype(BF16) for kv in kvs]
    o = [_dot(probs[kv], v[kv]) for kv in kvs]
    for kv in kvs:
        for g, h in enumerate(heads(kv)):
            o_ref[:, cols(h)] = o[kv][g * L:(g + 1) * L, :].astype(o_ref.dtype)


def _attention(proj, sinks, q_norm_w, k_norm_w, B, S):
    nb = S // WINDOW
    L = WINDOW
    row = lambda b, n: b * nb + n
    prev = lambda b, n: b * nb + jnp.maximum(n - 1, 0)
    return pl.pallas_call(
        _attn_kernel,
        out_shape=jax.ShapeDtypeStruct((B * S, ATTN_Q_W), BF16),
        grid=(B, nb),
        in_specs=[
            pl.BlockSpec(memory_space=pltpu.SMEM),
            pl.BlockSpec((L, ATTN_Q_W), lambda b, n: (row(b, n), 0)),
            pl.BlockSpec((L, ATTN_KV_W), lambda b, n: (prev(b, n), ATTN_K_BLK)),
            pl.BlockSpec((L, ATTN_KV_W), lambda b, n: (row(b, n), ATTN_K_BLK)),
            pl.BlockSpec((L, ATTN_KV_W), lambda b, n: (prev(b, n), ATTN_V_BLK)),
            pl.BlockSpec((L, ATTN_KV_W), lambda b, n: (row(b, n), ATTN_V_BLK)),
            pl.BlockSpec((1, ATTN_HEAD_DIM), lambda b, n: (0, 0)),
            pl.BlockSpec((1, ATTN_HEAD_DIM), lambda b, n: (0, 0)),
        ],
        out_specs=pl.BlockSpec((L, ATTN_Q_W), lambda b, n: (row(b, n), 0)),
        compiler_params=pltpu.CompilerParams(dimension_semantics=("arbitrary", "arbitrary")),
        name="attn",
    )(sinks, proj, proj, proj, proj, proj, q_norm_w.reshape(1, -1), k_norm_w.reshape(1, -1))


CONV_PAD = 8


DN_GROUP = 16

_X3 = ((0, 0), (1, 0), (0, 1))
_EXACT_LHS = ((0, 0), (0, 1), (0, 2))
_EXACT_RHS = ((0, 0), (1, 0), (2, 0))


def _split(x, n):
    parts = []
    for i in range(n):
        p = x.astype(BF16)
        parts.append(p)
        if i + 1 < n:
            x = x - p.astype(F32)
    return parts


def _mm(a_parts, b_parts, terms, form='nn'):
    a_axis = 0 if form == 'tn' else 1
    b_axis = 1 if form == 'nt' else 0
    a = jnp.concatenate([a_parts[i] for i, _ in terms], axis=a_axis)
    b = jnp.concatenate([b_parts[j] for _, j in terms], axis=b_axis)
    return {'nn': _dot, 'nt': _dot_nt, 'tn': _dot_tn}[form](a, b)


def _dn_front(h, alog_ref, dtb_ref, q_ref, k_ref, v_ref, ba_ref, cq_ref, ck_ref, cv_ref,
              xp, qs, ks, vs, gs, bs, us, ws, am, pm, qm, gl):
    S = q_ref.shape[0]
    C = DN_CHUNK
    dk = DN_HEAD_DIM
    piece = min(S, 256)

    def conv_silu(x_ref, cw_ref, dst, l2):
        xp[0:CONV_PAD, :] = jnp.zeros((CONV_PAD, dk), F32)
        xp[CONV_PAD:CONV_PAD + S, :] = x_ref[...]
        for p in range(S // piece):
            r0 = p * piece
            acc = jnp.zeros((piece, dk), F32)
            for i in range(CONV_WIDTH):
                off = CONV_PAD + r0 - (CONV_WIDTH - 1) + i
                acc = acc + xp[off:off + piece, :] * cw_ref[i:i + 1, :]
            y = acc * jax.nn.sigmoid(acc)
            if l2:
                y = y * lax.rsqrt(jnp.sum(y * y, axis=-1, keepdims=True) + L2_EPS)
            dst[r0:r0 + piece, :] = y

    conv_silu(q_ref, cq_ref, qs, True)
    conv_silu(k_ref, ck_ref, ks, True)
    conv_silu(v_ref, cv_ref, vs, False)

    neg_a = -jnp.exp(jnp.full((1, dk), alog_ref[h], F32))
    dtb = dtb_ref[h]
    for p in range(S // piece):
        r0 = p * piece
        ba = ba_ref[r0:r0 + piece, :]
        lane = lax.broadcasted_iota(jnp.int32, ba.shape, 1)
        b_col = jnp.sum(jnp.where(lane == h, ba, 0.0), axis=-1, keepdims=True)
        a_col = jnp.sum(jnp.where(lane == DN_HEADS + h, ba, 0.0), axis=-1, keepdims=True)
        bs[r0:r0 + piece, :] = jnp.broadcast_to(jax.nn.sigmoid(b_col), (piece, dk))
        gs[r0:r0 + piece, :] = neg_a * jnp.broadcast_to(jax.nn.softplus(a_col + dtb), (piece, dk))

    ri = lax.broadcasted_iota(jnp.int32, (C, C), 0)
    ci = lax.broadcasted_iota(jnp.int32, (C, C), 1)
    incl = ri >= ci
    strict = ri > ci
    tri_ones = [jnp.where(incl, 1.0, 0.0).astype(BF16)]
    upper_ones = [jnp.where(ri <= ci, 1.0, 0.0).astype(BF16)]
    n_chunks = S // C
    group = DN_GROUP if n_chunks % DN_GROUP == 0 else 1

    def intra_load(c):
        rows = pl.ds(pl.multiple_of(c * C, C), C)
        return qs[rows, :], ks[rows, :], vs[rows, :], gs[rows, :], bs[rows, :]

    def intra_compute(loaded):
        each = lambda f, *ls: [f(*a) for a in zip(*ls)]
        q, k, v, g, beta = (list(t) for t in zip(*loaded))
        q = each(lambda t: t * (dk ** -0.5), q)
        g3 = each(lambda t: _split(t, 3), g)
        gc = each(lambda p: _mm(tri_ones, p, _EXACT_LHS), g3)
        gc_row = each(lambda p: _mm([t[:, :C] for t in p], upper_ones, _EXACT_RHS, 'tn'), g3)
        decay = each(lambda a, b: jnp.exp(jnp.where(incl, a[:, :C] - b, -jnp.inf)), gc, gc_row)
        kb = each(jnp.multiply, k, beta)
        k2 = each(lambda t: _split(t, 2), k)
        kk = each(lambda a, b: _mm(_split(a, 2), b, _X3, 'nt'), kb, k2)
        qk = each(lambda a, b: _mm(_split(a, 2), b, _X3, 'nt'), q, k2)
        m = each(lambda a, d: -jnp.where(strict, a * d, 0.0), kk, decay)
        x = each(lambda vv, bb, kbb, gg: jnp.concatenate([vv * bb, kbb * jnp.exp(gg)], axis=1),
                 v, beta, kb, gc)
        n_fac = int(math.log2(C))
        for it in range(n_fac):
            m2 = each(lambda t: _split(t, 2), m)
            x = each(lambda xx, mm: xx + _mm(mm, _split(xx, 2), _X3), x, m2)
            if it + 1 < n_fac:
                m = each(lambda mm: _mm(mm, mm, _X3), m2)
        kd2 = each(lambda kk_, gg: _split(kk_ * jnp.exp(gg[C - 1:C, :] - gg), 2), k, gc)
        pq = each(lambda a, xx: _mm(a, _split(xx, 2), _X3, 'tn'), kd2, x)
        a_intra = each(lambda a, d: jnp.where(incl, a * d, 0.0), qk, decay)
        qg = each(lambda a, gg: a * jnp.exp(gg), q, gc)
        e_last = each(lambda gg: jnp.broadcast_to(jnp.exp(gg[C - 1:C, :]), (8, dk)), gc)
        return list(zip(x, qg, a_intra, pq, e_last))

    def intra_store(c, x, qg, a_intra, pq, e_last):
        rows = pl.ds(pl.multiple_of(c * C, C), C)
        prow = pl.ds(pl.multiple_of(c * dk, dk), dk)
        us[rows, :] = x[:, :dk]
        ws[rows, :] = x[:, dk:]
        qs[rows, :] = qg
        am[rows, :] = a_intra
        qm[prow, :] = pq[:, :dk]
        pm[prow, :] = pq[:, dk:]
        gl[pl.ds(pl.multiple_of(c * 8, 8), 8), :] = e_last

    def intra_group(i, _):
        loaded = [intra_load(i * group + t) for t in range(group)]
        outs = intra_compute(loaded)
        for t in range(group):
            intra_store(i * group + t, *outs[t])
        return 0

    lax.fori_loop(0, n_chunks // group, intra_group, 0)


def _dn_scan(heads):
    dk = DN_HEAD_DIM
    n_chunks = heads[0][0].shape[0] // dk

    def scan(c, states):
        prow = pl.ds(pl.multiple_of(c * dk, dk), dk)
        ps = [_mm(_split(pm[prow, :], 2), _split(state, 2), _X3) for (pm, _, _), state in zip(heads, states)]
        for (pm, _, _), state in zip(heads, states):
            pm[prow, :] = state
        e_last = [gl[pl.ds(pl.multiple_of(c * 8, 8), 1), :] for _, _, gl in heads]
        return tuple(state * e - p + qm[prow, :]
                     for (_, qm, _), state, e, p in zip(heads, states, e_last, ps))

    lax.fori_loop(0, n_chunks, scan, tuple(jnp.zeros((dk, dk), F32) for _ in heads), unroll=2)


def _dn_back(z_ref, nw_ref, o_ref, qs, us, ws, am, pm):
    S = z_ref.shape[0]
    C = DN_CHUNK
    dk = DN_HEAD_DIM
    n_chunks = S // C
    group = DN_GROUP if n_chunks % DN_GROUP == 0 else 1
    nw = nw_ref[...]

    def out_group(i, _):
        cs = [i * group + t for t in range(group)]
        rows = [pl.ds(pl.multiple_of(c * C, C), C) for c in cs]
        each = lambda f, *ls: [f(*a) for a in zip(*ls)]
        state2 = each(lambda c: _split(pm[pl.ds(pl.multiple_of(c * dk, dk), dk), :], 2), cs)
        lhs2 = each(lambda r: _split(jnp.concatenate([ws[r, :], qs[r, :]], axis=0), 2), rows)
        prod = each(lambda a, b: _mm(a, b, _X3), lhs2, state2)
        v_new = each(lambda r, p: us[r, :] - p[:C], rows, prod)
        av = each(lambda r, vn: _mm(_split(am[r, :], 2), _split(vn, 2), _X3), rows, v_new)
        o = each(lambda p, a: p[C:] + a, prod, av)
        o = each(lambda t: t * lax.rsqrt(jnp.mean(t * t, axis=-1, keepdims=True) + DN_NORM_EPS) * nw, o)
        for r, t in zip(rows, o):
            z = z_ref[r, :]
            o_ref[r, :] = (t * (z * jax.nn.sigmoid(z))).astype(o_ref.dtype)
        return 0

    lax.fori_loop(0, n_chunks // group, out_group, 0)


DN_HEADS_PER_STEP = 2


def _dn_kernel(alog_ref, dtb_ref, q_ref, k_ref, v_ref, z_ref, ba_ref, cq_ref, ck_ref, cv_ref,
               nw_ref, o_ref, xp, *scratch):
    dk = DN_HEAD_DIM
    n_heads = o_ref.shape[1] // dk
    lanes = lambda ref, hh: ref.at[pl.ds(0, ref.shape[0]), pl.ds(hh * dk, dk)]
    views = [tuple(r.at[hh] for r in scratch) for hh in range(n_heads)]
    for hh in range(n_heads):
        _dn_front(pl.program_id(1) * n_heads + hh, alog_ref, dtb_ref, lanes(q_ref, hh), lanes(k_ref, hh),
                  lanes(v_ref, hh), ba_ref, lanes(cq_ref, hh), lanes(ck_ref, hh), lanes(cv_ref, hh),
                  xp, *views[hh])
    _dn_scan([(pm, qm, gl) for (qs, ks, vs, gs, bs, us, ws, am, pm, qm, gl) in views])
    for hh, (qs, ks, vs, gs, bs, us, ws, am, pm, qm, gl) in enumerate(views):
        _dn_back(lanes(z_ref, hh), nw_ref, lanes(o_ref, hh), qs, us, ws, am, pm)


def _deltanet(proj, conv_w, a_log, dt_bias, dn_norm_w, B, S):
    dk = DN_HEAD_DIM
    hp = DN_HEADS_PER_STEP
    seq = lambda blk: pl.BlockSpec((S, hp * dk), lambda b, h: (b, blk // hp + h))
    cw = lambda blk: pl.BlockSpec((CONV_WIDTH, hp * dk), lambda b, h: (0, blk // hp + h))
    assert all(blk % hp == 0 for blk in (DN_Q_BLK, DN_K_BLK, DN_V_BLK, DN_Z_BLK, DN_HEADS))
    return pl.pallas_call(
        _dn_kernel,
        out_shape=jax.ShapeDtypeStruct((B * S, DN_W), BF16),
        grid=(B, DN_HEADS // hp),
        in_specs=[
            pl.BlockSpec(memory_space=pltpu.SMEM),
            pl.BlockSpec(memory_space=pltpu.SMEM),
            seq(DN_Q_BLK), seq(DN_K_BLK), seq(DN_V_BLK), seq(DN_Z_BLK),
            pl.BlockSpec((S, LANES), lambda b, h: (b, DN_BA_BLK)),
            cw(0), cw(DN_HEADS), cw(2 * DN_HEADS),
            pl.BlockSpec((1, dk), lambda b, h: (0, 0)),
        ],
        out_specs=pl.BlockSpec((S, hp * dk), lambda b, h: (b, h)),
        scratch_shapes=[pltpu.VMEM((S + CONV_PAD, dk), F32)] + [pltpu.VMEM((hp, S, dk), F32)] * 7
        + [pltpu.VMEM((hp, S, DN_CHUNK), F32)] + [pltpu.VMEM((hp, S // DN_CHUNK * dk, dk), F32)] * 2
        + [pltpu.VMEM((hp, S // DN_CHUNK * 8, dk), F32)],
        compiler_params=pltpu.CompilerParams(
            dimension_semantics=("arbitrary", "arbitrary"), vmem_limit_bytes=58 * MIB),
        name="deltanet",
    )(a_log, dt_bias, proj, proj, proj, proj, proj, conv_w, conv_w, conv_w,
      dn_norm_w.reshape(1, dk))


def _pack_bf16_pairs(x):
    c = x.shape[1] // 2
    bits = lambda t: lax.bitcast_convert_type(t.astype(BF16).astype(F32), jnp.uint32)
    return (bits(x[:, :c]) >> 16) | (bits(x[:, c:]) & jnp.uint32(0xFFFF0000))


def _unpack_bf16_pairs(p):
    lo = lax.bitcast_convert_type(p << 16, F32).astype(BF16)
    hi = lax.bitcast_convert_type(p & jnp.uint32(0xFFFF0000), F32).astype(BF16)
    return lo, hi


def _outproj_kernel(a_ref, d_ref, x_ref, wa_ref, wd_ref, nw_ref, wrh_ref, wrl_ref, br_ref,
                    h_ref, u_ref, ti_ref, tg_ref):
    h = x_ref[...] + _dot(a_ref[...], wa_ref[...]) + _dot(d_ref[...], wd_ref[...])
    h_ref[...] = h
    ms = jnp.mean(h * h, axis=-1, keepdims=True)
    u = h * lax.rsqrt(ms + NORM_EPS) * nw_ref[...]
    u_ref[...] = _pack_bf16_pairs(u)
    u_hi, u_lo = _split(u, 2)
    logits = (_dot(u_hi, wrh_ref[...]) + _dot(u_lo, wrh_ref[...]) + _dot(u_hi, wrl_ref[...])
              + br_ref[...])
    lane = lax.broadcasted_iota(jnp.int32, logits.shape, 1)
    lane_f = lane.astype(F32)
    vals = jnp.where(lane < N_EXPERTS, logits, -jnp.inf)
    top_v, top_i = [], []
    for _ in range(TOP_K):
        m = jnp.max(vals, axis=-1, keepdims=True)
        idx = jnp.min(jnp.where(vals == m, lane_f, float(LANES)), axis=-1, keepdims=True)
        top_v.append(m)
        top_i.append(idx)
        vals = jnp.where(lane_f == idx, -jnp.inf, vals)
    e = [jnp.exp(v - top_v[0]) for v in top_v]
    denom = e[0] + e[1] + e[2] + e[3]
    ti = jnp.zeros(logits.shape, F32)
    tg = jnp.zeros(logits.shape, F32)
    for k in range(TOP_K):
        ti = jnp.where(lane == k, top_i[k], ti)
        tg = jnp.where(lane == k, e[k] / denom, tg)
    ti_ref[...] = ti.astype(jnp.int32)
    tg_ref[...] = tg


def _outproj_router(attn_o, dn_o, x2d, w_out, ffn_norm_w, w_router, b_router, tm):
    T, D = x2d.shape
    wa = w_out[:ATTN_Q_W].astype(BF16)
    wd = w_out[ATTN_Q_W:].astype(BF16)
    wr = jnp.zeros((D, LANES), F32).at[:, :N_EXPERTS].set(w_router)
    wr_hi = wr.astype(BF16)
    wr_lo = (wr - wr_hi.astype(F32)).astype(BF16)
    br = jnp.zeros((1, LANES), F32).at[0, :N_EXPERTS].set(b_router)
    const = lambda shape: pl.BlockSpec(shape, lambda i: (0, 0))
    tile = lambda w: pl.BlockSpec((tm, w), lambda i: (i, 0))
    return pl.pallas_call(
        _outproj_kernel,
        out_shape=(jax.ShapeDtypeStruct((T, D), F32), jax.ShapeDtypeStruct((T, D // 2), jnp.uint32),
                   jax.ShapeDtypeStruct((T, LANES), jnp.int32),
                   jax.ShapeDtypeStruct((T, LANES), F32)),
        grid=(T // tm,),
        in_specs=[tile(ATTN_Q_W), tile(DN_W), tile(D), const((ATTN_Q_W, D)), const((DN_W, D)),
                  const((1, D)), const((D, LANES)), const((D, LANES)), const((1, LANES))],
        out_specs=(tile(D), tile(D // 2), tile(LANES), tile(LANES)),
        compiler_params=pltpu.CompilerParams(
            dimension_semantics=("arbitrary",), vmem_limit_bytes=56 * MIB),
        name="outproj",
    )(attn_o, dn_o, x2d, wa, wd, ffn_norm_w.reshape(1, D), wr_hi, wr_lo, br)


def _experts_kernel(ie_ref, ib_ref, in_ref, used_ref, x_ref, wg_ref, wl_ref, wd_ref,
                    bg_ref, bl_ref, bd_ref, y_ref, xraw, ybuf, wg_s, wl_s, wd_s, sem_x, sem_y):
    w = pl.program_id(0)
    j = pl.program_id(1)
    nw = pl.num_programs(0)
    nj = pl.num_programs(1)
    R = MOE_ROWS
    nb = in_ref[w]
    row0 = ib_ref[w] * R
    slot = lax.rem(w, 2)
    w_next = jnp.minimum(w + 1, nw - 1)
    nb_next = jnp.where(w + 1 < nw, in_ref[w_next], 0)
    row0_next = ib_ref[w_next] * R

    def x_copy(base, s, r):
        return pltpu.make_async_copy(x_ref.at[pl.ds(base + r * R, R), :],
                                     xraw.at[s, pl.ds(r * R, R), :], sem_x.at[s])

    def y_copy(r):
        return pltpu.make_async_copy(ybuf.at[pl.ds(r * R, R), :],
                                     y_ref.at[pl.ds(row0 + r * R, R), :], sem_y)

    def for_range(n, fn):
        def body(r, _):
            fn(r)
            return 0
        lax.fori_loop(0, n, body, 0)

    def rows_of(r):
        return pl.ds(pl.multiple_of(r * R, R), R)

    @pl.when((j == 0) & (w == 0))
    def _():
        for_range(nb, lambda r: x_copy(row0, slot, r).start())

    @pl.when(j == 0)
    def _():
        bd = jnp.broadcast_to(bd_ref[...], (R, ybuf.shape[1]))

        def init(r):
            ybuf[rows_of(r), :] = bd

        for_range(nb, init)
        for_range(nb, lambda r: x_copy(row0, slot, r).wait())
        for_range(nb_next, lambda r: x_copy(row0_next, 1 - slot, r).start())

    @pl.when(nb > 0)
    def _():
        wg_s[...] = wg_ref[...].astype(BF16)
        wl_s[...] = wl_ref[...].astype(BF16)
        wd_s[...] = wd_ref[...].astype(BF16)

    bg = bg_ref[...]
    bl = bl_ref[...]

    def blocks(rs, write_back):
        rows = [rows_of(r) for r in rs]
        xs = [jnp.concatenate(_unpack_bf16_pairs(xraw[slot, rw, :]), axis=1) for rw in rows]
        hg = [jnp.minimum(_dot(x, wg_s[...]) + bg, SWIGLU_LIMIT) for x in xs]
        hl = [jnp.clip(_dot(x, wl_s[...]) + bl, -SWIGLU_LIMIT, SWIGLU_LIMIT) for x in xs]
        acts = [(a * jax.nn.sigmoid(SWIGLU_ALPHA * a) * (b + 1.0)).astype(BF16) for a, b in zip(hg, hl)]
        for rw, act in zip(rows, acts):
            ybuf[rw, :] += _dot(act, wd_s[...])
        if write_back:
            for r in rs:
                y_copy(r).start()

    def all_blocks(write_back):
        lock = MOE_LOCKSTEP_BLOCKS

        def full(i, _):
            blocks([lock * i + t for t in range(lock)], write_back)
            return 0

        lax.fori_loop(0, nb // lock, full, 0)
        base = (nb // lock) * lock
        size = lock // 2
        while size >= 1:
            @pl.when(((nb - base) & size) != 0)
            def _(start=base, size=size):
                blocks([start + t for t in range(size)], write_back)

            base = base + ((nb - base) & size)
            size //= 2

    @pl.when(j < nj - 1)
    def _():
        all_blocks(False)

    @pl.when(j == nj - 1)
    def _():
        all_blocks(True)
        for_range(nb, lambda r: y_copy(r).wait())

    @pl.when((j == nj - 1) & (w == pl.num_programs(0) - 1))
    def _():
        ybuf[0:R, :] = jnp.zeros((R, ybuf.shape[1]), F32)

        def tail_copy(r):
            return pltpu.make_async_copy(ybuf.at[pl.ds(0, R), :], y_ref.at[pl.ds(r * R, R), :], sem_y)

        def tail(fn):
            def body(r, _):
                fn(r)
                return 0
            lax.fori_loop(used_ref[0], y_ref.shape[0] // R, body, 0)

        tail(lambda r: tail_copy(r).start())
        tail(lambda r: tail_copy(r).wait())


def _experts(x_sorted, item_expert, item_blk0, item_nblk, n_blocks_used,
             w_gate_up, b_gate_up, w_down, b_down):
    P = x_sorted.shape[0]
    D = x_sorted.shape[1] * 2
    E, _, F2 = w_gate_up.shape
    F = F2 // 2
    tf = MOE_F_TILE
    nj = F // tf
    n_items = item_expert.shape[0]
    rows = MOE_ITEM_BLOCKS * MOE_ROWS
    b_gu = b_gate_up.reshape(E, 1, F2)
    b_d = b_down.reshape(E, 1, D)
    jt = lambda j, n: jnp.where(n > 0, j, nj - 1)
    return pl.pallas_call(
        _experts_kernel,
        out_shape=jax.ShapeDtypeStruct((P, D), F32),
        grid_spec=pltpu.PrefetchScalarGridSpec(
            num_scalar_prefetch=4,
            grid=(n_items, nj),
            in_specs=[
                pl.BlockSpec(memory_space=pl.ANY),
                pl.BlockSpec((None, D, tf), lambda w, j, ie, ib, nb, *_: (ie[w], 0, jt(j, nb[w]))),
                pl.BlockSpec((None, D, tf), lambda w, j, ie, ib, nb, *_: (ie[w], 0, nj + jt(j, nb[w]))),
                pl.BlockSpec((None, tf, D), lambda w, j, ie, ib, nb, *_: (ie[w], jt(j, nb[w]), 0)),
                pl.BlockSpec((None, 1, tf), lambda w, j, ie, ib, nb, *_: (ie[w], 0, jt(j, nb[w]))),
                pl.BlockSpec((None, 1, tf), lambda w, j, ie, ib, nb, *_: (ie[w], 0, nj + jt(j, nb[w]))),
                pl.BlockSpec((None, 1, D), lambda w, j, ie, *_: (ie[w], 0, 0)),
            ],
            out_specs=pl.BlockSpec(memory_space=pl.ANY),
            scratch_shapes=[
                pltpu.VMEM((2, rows, D // 2), jnp.uint32), pltpu.VMEM((rows, D), F32),
                pltpu.VMEM((D, tf), BF16), pltpu.VMEM((D, tf), BF16), pltpu.VMEM((tf, D), BF16),
                pltpu.SemaphoreType.DMA((2,)), pltpu.SemaphoreType.DMA,
            ],
        ),
        compiler_params=pltpu.CompilerParams(
            dimension_semantics=("arbitrary", "arbitrary"), vmem_limit_bytes=60 * MIB),
        name="experts",
    )(item_expert, item_blk0, item_nblk, n_blocks_used, x_sorted,
      w_gate_up, w_gate_up, w_down, b_gu, b_gu, b_d)


def _combine_kernel(dest_ref, h_ref, tg_ref, y_ref, o_ref, buf, sem):
    i = pl.program_id(0)
    n = pl.num_programs(0)
    tm = h_ref.shape[0]

    def issue_tile(step):
        slot = lax.rem(step, 2)

        def issue(r, _):
            for t in range(ROWS_PER_ISSUE):
                for k in range(TOP_K):
                    row = dest_ref[(step * tm + ROWS_PER_ISSUE * r + t) * TOP_K + k]
                    pltpu.make_async_copy(y_ref.at[pl.ds(row, 1), :],
                                          buf.at[slot, k, pl.ds(ROWS_PER_ISSUE * r + t, 1), :],
                                          sem.at[slot]).start()
            return 0

        lax.fori_loop(0, tm // ROWS_PER_ISSUE, issue, 0)

    @pl.when(i == 0)
    def _():
        issue_tile(0)

    @pl.when(i + 1 < n)
    def _():
        issue_tile(i + 1)

    slot = lax.rem(i, 2)
    for k in range(TOP_K):
        pltpu.make_async_copy(y_ref.at[pl.ds(0, tm), :], buf.at[slot, k], sem.at[slot]).wait()
    tg = tg_ref[...]
    acc = h_ref[...]
    for k in range(TOP_K):
        acc = acc + tg[:, k:k + 1] * buf[slot, k]
    o_ref[...] = acc


def _combine(h, top_g, y_sorted, dest, tm):
    T, D = h.shape
    return pl.pallas_call(
        _combine_kernel,
        out_shape=jax.ShapeDtypeStruct((T, D), F32),
        grid_spec=pltpu.PrefetchScalarGridSpec(
            num_scalar_prefetch=1,
            grid=(T // tm,),
            in_specs=[
                pl.BlockSpec((tm, D), lambda i, d: (i, 0)),
                pl.BlockSpec((tm, LANES), lambda i, d: (i, 0)),
                pl.BlockSpec(memory_space=pl.ANY),
            ],
            out_specs=pl.BlockSpec((tm, D), lambda i, d: (i, 0)),
            scratch_shapes=[pltpu.VMEM((2, TOP_K, tm, D), F32), pltpu.SemaphoreType.DMA((2,))],
        ),
        compiler_params=pltpu.CompilerParams(dimension_semantics=("arbitrary",)),
        name="combine",
    )(dest, h, top_g, y_sorted)


def _dispatch_kernel(dest_ref, row_end_ref, blk_end_ref, u_ref, x_ref, buf, zbuf, sem_l, sem_s, sem_z):
    i = pl.program_id(0)
    n = pl.num_programs(0)
    tm = buf.shape[1]
    R = MOE_ROWS
    n_blocks = x_ref.shape[0] // R

    def load(step):
        s = lax.rem(step, 2)
        return pltpu.make_async_copy(u_ref.at[pl.ds(step * tm, tm), :], buf.at[s], sem_l.at[s])

    def zero_copy(blk):
        return pltpu.make_async_copy(zbuf, x_ref.at[pl.ds(blk * R, R), :], sem_z)

    def wait_scatter(step):
        s = lax.rem(step, 2)
        for _ in range(TOP_K):
            pltpu.make_async_copy(u_ref.at[pl.ds(0, tm), :], buf.at[s], sem_s.at[s]).wait()

    @pl.when(i == 0)
    def _():
        zbuf[...] = jnp.zeros(zbuf.shape, zbuf.dtype)

        def partial_blocks(fn):
            def body(e, _):
                @pl.when(lax.rem(row_end_ref[e], R) != 0)
                def _():
                    fn(blk_end_ref[e] - 1)
                return 0
            lax.fori_loop(0, N_EXPERTS, body, 0)

        def unused_blocks(fn):
            def body(b, _):
                fn(b)
                return 0
            lax.fori_loop(blk_end_ref[N_EXPERTS - 1], n_blocks, body, 0)

        partial_blocks(lambda b: zero_copy(b).start())
        unused_blocks(lambda b: zero_copy(b).start())
        partial_blocks(lambda b: zero_copy(b).wait())
        unused_blocks(lambda b: zero_copy(b).wait())
        load(0).start()

    @pl.when(i > 0)
    def _():
        wait_scatter(i - 1)

    @pl.when(i + 1 < n)
    def _():
        load(i + 1).start()

    load(i).wait()
    slot = lax.rem(i, 2)

    def issue(r, _):
        for t in range(ROWS_PER_ISSUE):
            for k in range(TOP_K):
                row = dest_ref[(i * tm + ROWS_PER_ISSUE * r + t) * TOP_K + k]
                pltpu.make_async_copy(buf.at[slot, pl.ds(ROWS_PER_ISSUE * r + t, 1), :],
                                      x_ref.at[pl.ds(row, 1), :], sem_s.at[slot]).start()
        return 0

    lax.fori_loop(0, tm // ROWS_PER_ISSUE, issue, 0)

    @pl.when(i == n - 1)
    def _():
        wait_scatter(i)


def _dispatch(u_packed, dest, row_end, blk_end, P, tm):
    T, W = u_packed.shape
    return pl.pallas_call(
        _dispatch_kernel,
        out_shape=jax.ShapeDtypeStruct((P, W), u_packed.dtype),
        grid_spec=pltpu.PrefetchScalarGridSpec(
            num_scalar_prefetch=3,
            grid=(T // tm,),
            in_specs=[pl.BlockSpec(memory_space=pl.ANY)],
            out_specs=pl.BlockSpec(memory_space=pl.ANY),
            scratch_shapes=[pltpu.VMEM((2, tm, W), u_packed.dtype), pltpu.VMEM((MOE_ROWS, W), u_packed.dtype),
                            pltpu.SemaphoreType.DMA((2,)), pltpu.SemaphoreType.DMA((2,)),
                            pltpu.SemaphoreType.DMA],
        ),
        compiler_params=pltpu.CompilerParams(dimension_semantics=("arbitrary",)),
        name="dispatch",
    )(dest, row_end, blk_end, u_packed)


def _routing(top_i, T):
    R = MOE_ROWS
    n_assign = T * TOP_K
    e_flat = top_i[:, :TOP_K].reshape(n_assign)
    onehot = (e_flat[:, None] == jnp.arange(N_EXPERTS, dtype=jnp.int32)[None, :]).astype(jnp.int32)
    csum = jnp.cumsum(onehot, axis=0)
    rank = jnp.take_along_axis(csum, e_flat[:, None], axis=1)[:, 0] - 1
    counts = csum[-1]
    nblk = (counts + R - 1) // R
    blk_end = jnp.cumsum(nblk)
    blk_start = blk_end - nblk
    dest = blk_start[e_flat] * R + rank
    max_blocks = n_assign // R + N_EXPERTS
    P = max_blocks * R
    dest = dest.astype(jnp.int32)
    row_end = (blk_start * R + counts).astype(jnp.int32)
    nb_item = MOE_ITEM_BLOCKS
    n_items_e = (nblk + nb_item - 1) // nb_item
    item_end = jnp.cumsum(n_items_e)
    item_start = item_end - n_items_e
    max_items = (max_blocks + (nb_item - 1) * N_EXPERTS) // nb_item
    w = jnp.arange(max_items, dtype=jnp.int32)
    ie = jnp.minimum(jnp.searchsorted(item_end, w, side='right'), N_EXPERTS - 1).astype(jnp.int32)
    local = w - item_start[ie]
    live = w < item_end[-1]
    ib = blk_start[ie] + local * nb_item
    inb = jnp.clip(nblk[ie] - local * nb_item, 0, nb_item)
    last_e = ie[jnp.maximum(item_end[-1] - 1, 0)]
    ie = jnp.where(live, ie, last_e).astype(jnp.int32)
    ib = jnp.where(live, ib, 0).astype(jnp.int32)
    inb = jnp.where(live, inb, 0).astype(jnp.int32)
    return dest, row_end, blk_end.astype(jnp.int32), P, ie, ib, inb


def _layer(x, attn_norm_w, w_in, q_norm_w, k_norm_w, attn_sinks, conv_w, a_log, dt_bias,
           dn_norm_w, w_out, ffn_norm_w, w_router, b_router, w_gate_up, b_gate_up, w_down, b_down):
    B, S, D = x.shape
    T = B * S
    x2d = x.reshape(T, D)
    proj = _inproj(x2d, attn_norm_w, w_in, tm=min(T, 1024), tn=1024)
    attn_o = _attention(proj, attn_sinks, q_norm_w, k_norm_w, B, S)
    dn_o = _deltanet(proj, conv_w, a_log, dt_bias, dn_norm_w, B, S)
    h, u, top_i, top_g = _outproj_router(attn_o, dn_o, x2d, w_out, ffn_norm_w, w_router, b_router,
                                         tm=min(T, 512))
    dest, row_end, blk_end, P, ie, ib, inb = _routing(top_i, T)
    x_sorted = _dispatch(u, dest, row_end, blk_end, P, tm=min(T, 256))
    y_sorted = _experts(x_sorted, ie, ib, inb, blk_end[-1:], w_gate_up, b_gate_up, w_down, b_down)
    out = _combine(h, top_g, y_sorted, dest, tm=min(T, 128))
    return out.reshape(B, S, D)


def kernel(x, attn_norm_w, w_in, q_norm_w, k_norm_w, attn_sinks, conv_w, a_log, dt_bias, dn_norm_w,
           w_out, ffn_norm_w, w_router, b_router, w_gate_up, b_gate_up, w_down, b_down):
    h = x
    for l in range(attn_norm_w.shape[0]):
        h = _layer(h, attn_norm_w[l], w_in[l], q_norm_w[l], k_norm_w[l], attn_sinks[l], conv_w[l],
                   a_log[l], dt_bias[l], dn_norm_w[l], w_out[l], ffn_norm_w[l], w_router[l],
                   b_router[l], w_gate_up[l], b_gate_up[l], w_down[l], b_down[l])
    return h
```

```python
import functools
import math

import jax
import jax.numpy as jnp
from jax import lax
from jax.experimental import pallas as pl
from jax.experimental.pallas import tpu as pltpu

F32 = jnp.float32
BF16 = jnp.bfloat16

D_MODEL = 2048
ATTN_HEAD_DIM = 64
ATTN_HEADS = 16
ATTN_KV_HEADS = 4
ATTN_GROUP = ATTN_HEADS // ATTN_KV_HEADS
WINDOW = 128
DN_HEAD_DIM = 128
DN_HEADS = 8
DN_CHUNK = 64
CONV_WIDTH = 4
ATTN_Q_W = ATTN_HEADS * ATTN_HEAD_DIM
ATTN_KV_W = ATTN_KV_HEADS * ATTN_HEAD_DIM
DN_W = DN_HEADS * DN_HEAD_DIM
IN_WIDTH = ATTN_Q_W + 2 * ATTN_KV_W + 4 * DN_W + 2 * DN_HEADS
N_EXPERTS = 32
TOP_K = 4
D_EXPERT = 2048
SWIGLU_ALPHA = 1.702
SWIGLU_LIMIT = 7.0
NORM_EPS = 1e-5
QK_NORM_EPS = 1e-6
DN_NORM_EPS = 1e-6
L2_EPS = 1e-6

LANES = 128
MIB = 1024 * 1024

ATTN_K_BLK = ATTN_Q_W // ATTN_KV_W
ATTN_V_BLK = ATTN_K_BLK + 1
DN_Q_BLK = (ATTN_Q_W + 2 * ATTN_KV_W) // DN_HEAD_DIM
DN_K_BLK = DN_Q_BLK + DN_HEADS
DN_V_BLK = DN_K_BLK + DN_HEADS
DN_Z_BLK = DN_V_BLK + DN_HEADS
DN_BA_BLK = DN_Z_BLK + DN_HEADS

MOE_ROWS = 256
MOE_ITEM_BLOCKS = 9
MOE_F_TILE = 256
MOE_LOCKSTEP_BLOCKS = 4
ROWS_PER_ISSUE = 4


def _dot(a, b):
    return jnp.dot(a, b, preferred_element_type=F32)


def _dot_nt(a, b):
    return lax.dot_general(a, b, (((1,), (1,)), ((), ())), preferred_element_type=F32)


def _dot_tn(a, b):
    return lax.dot_general(a, b, (((0,), (0,)), ((), ())), preferred_element_type=F32)


def _inproj_kernel(x_ref, nw_ref, w_ref, o_ref, u_ref, *, last_width):
    @pl.when(pl.program_id(1) == 0)
    def _():
        x = x_ref[...]
        ms = jnp.mean(x * x, axis=-1, keepdims=True)
        u_ref[...] = (x * lax.rsqrt(ms + NORM_EPS) * nw_ref[...]).astype(BF16)

    last = pl.num_programs(1) - 1

    @pl.when(pl.program_id(1) < last)
    def _():
        o_ref[...] = _dot(u_ref[...], w_ref[...])

    @pl.when(pl.program_id(1) == last)
    def _():
        o_ref[:, :last_width] = _dot(u_ref[...], w_ref[:, :last_width])


def _inproj(x2d, norm_w, w_in, tm, tn):
    T, D = x2d.shape
    N = w_in.shape[1]
    n_col_blocks = pl.cdiv(N, tn)
    last_width = min(tn, -(-(N - (n_col_blocks - 1) * tn) // LANES) * LANES)
    return pl.pallas_call(
        functools.partial(_inproj_kernel, last_width=last_width),
        out_shape=jax.ShapeDtypeStruct((T, N), F32),
        grid=(T // tm, pl.cdiv(N, tn)),
        in_specs=[
            pl.BlockSpec((tm, D), lambda i, j: (i, 0)),
            pl.BlockSpec((1, D), lambda i, j: (0, 0)),
            pl.BlockSpec((D, tn), lambda i, j: (0, j)),
        ],
        out_specs=pl.BlockSpec((tm, tn), lambda i, j: (i, j)),
        scratch_shapes=[pltpu.VMEM((tm, D), BF16)],
        compiler_params=pltpu.CompilerParams(
            dimension_semantics=("arbitrary", "arbitrary"), vmem_limit_bytes=56 * MIB),
        name="inproj",
    )(x2d, norm_w.reshape(1, D), w_in.astype(BF16))


def _attn_kernel(sink_ref, q_ref, kp_ref, kc_ref, vp_ref, vc_ref, qw_ref, kw_ref, o_ref):
    n = pl.program_id(1)
    L = WINDOW
    d = ATTN_HEAD_DIM

    def head_norm(t, w):
        ms = jnp.mean(t * t, axis=-1, keepdims=True)
        return t * lax.rsqrt(ms + QK_NORM_EPS) * w

    G = ATTN_GROUP
    qi = lax.broadcasted_iota(jnp.int32, (G * L, 2 * L), 0)
    kj = lax.broadcasted_iota(jnp.int32, (G * L, 2 * L), 1)
    dist = (qi & (L - 1)) + L - kj
    valid = (dist >= 0) & (dist < WINDOW) & ((kj >= L) | (n > 0))
    dist_f = dist.astype(F32)
    grp = lax.broadcasted_iota(jnp.int32, (G * L, 1), 0) >> (L.bit_length() - 1)

    def per_row(vals):
        col = jnp.full((G * L, 1), vals[G - 1], F32)
        for g in range(G - 2, -1, -1):
            col = jnp.where(grp == g, vals[g], col)
        return col

    qw = qw_ref[...]
    kw = kw_ref[...]
    kvs = range(ATTN_KV_HEADS)
    heads = lambda kv: [kv * G + g for g in range(G)]
    cols = lambda h: slice(h * d, (h + 1) * d)
    k = [head_norm(jnp.concatenate([kp_ref[:, cols(kv)], kc_ref[:, cols(kv)]], axis=0), kw).astype(BF16)
         for kv in kvs]
    v = [jnp.concatenate([vp_ref[:, cols(kv)], vc_ref[:, cols(kv)]], axis=0).astype(BF16) for kv in kvs]
    q = [head_norm(jnp.concatenate([q_ref[:, cols(h)] for h in heads(kv)], axis=0), qw).astype(BF16)
         for kv in kvs]
    s = [_dot_nt(q[kv], k[kv]) * (d ** -0.5) for kv in kvs]
    slope = [per_row([2.0 ** (-8.0 * (h + 1) / ATTN_HEADS) for h in heads(kv)]) for kv in kvs]
    sink = [per_row([sink_ref[h] for h in heads(kv)]) for kv in kvs]
    s = [jnp.where(valid, s[kv] - slope[kv] * dist_f, -jnp.inf) for kv in kvs]
    m = [jnp.maximum(jnp.max(s[kv], axis=-1, keepdims=True), sink[kv]) for kv in kvs]
    p = [jnp.exp(s[kv] - m[kv]) for kv in kvs]
    denom = [jnp.sum(p[kv], axis=-1, keepdims=True) + jnp.exp(sink[kv] - m[kv]) for kv in kvs]
    probs = [(p[kv] / denom[kv]).astype(BF16) for kv in kvs]
    o = [_dot(probs[kv], v[kv]) for kv in kvs]
    for kv in kvs:
        for g, h in enumerate(heads(kv)):
            o_ref[:, cols(h)] = o[kv][g * L:(g + 1) * L, :].astype(o_ref.dtype)


def _attention(proj, sinks, q_norm_w, k_norm_w, B, S):
    nb = S // WINDOW
    L = WINDOW
    row = lambda b, n: b * nb + n
    prev = lambda b, n: b * nb + jnp.maximum(n - 1, 0)
    return pl.pallas_call(
        _attn_kernel,
        out_shape=jax.ShapeDtypeStruct((B * S, ATTN_Q_W), BF16),
        grid=(B, nb),
        in_specs=[
            pl.BlockSpec(memory_space=pltpu.SMEM),
            pl.BlockSpec((L, ATTN_Q_W), lambda b, n: (row(b, n), 0)),
            pl.BlockSpec((L, ATTN_KV_W), lambda b, n: (prev(b, n), ATTN_K_BLK)),
            pl.BlockSpec((L, ATTN_KV_W), lambda b, n: (row(b, n), ATTN_K_BLK)),
            pl.BlockSpec((L, ATTN_KV_W), lambda b, n: (prev(b, n), ATTN_V_BLK)),
            pl.BlockSpec((L, ATTN_KV_W), lambda b, n: (row(b, n), ATTN_V_BLK)),
            pl.BlockSpec((1, ATTN_HEAD_DIM), lambda b, n: (0, 0)),
            pl.BlockSpec((1, ATTN_HEAD_DIM), lambda b, n: (0, 0)),
        ],
        out_specs=pl.BlockSpec((L, ATTN_Q_W), lambda b, n: (row(b, n), 0)),
        compiler_params=pltpu.CompilerParams(dimension_semantics=("arbitrary", "arbitrary")),
        name="attn",
    )(sinks, proj, proj, proj, proj, proj, q_norm_w.reshape(1, -1), k_norm_w.reshape(1, -1))


CONV_PAD = 8


DN_GROUP = 16

_X3 = ((0, 0), (1, 0), (0, 1))
_EXACT_LHS = ((0, 0), (0, 1), (0, 2))
_EXACT_RHS = ((0, 0), (1, 0), (2, 0))


def _split(x, n):
    parts = []
    for i in range(n):
        p = x.astype(BF16)
        parts.append(p)
        if i + 1 < n:
            x = x - p.astype(F32)
    return parts


def _mm(a_parts, b_parts, terms, form='nn'):
    a_axis = 0 if form == 'tn' else 1
    b_axis = 1 if form == 'nt' else 0
    a = jnp.concatenate([a_parts[i] for i, _ in terms], axis=a_axis)
    b = jnp.concatenate([b_parts[j] for _, j in terms], axis=b_axis)
    return {'nn': _dot, 'nt': _dot_nt, 'tn': _dot_tn}[form](a, b)


def _dn_front(h, alog_ref, dtb_ref, q_ref, k_ref, v_ref, ba_ref, cq_ref, ck_ref, cv_ref,
              xp, qs, ks, vs, gs, bs, us, ws, am, pm, qm, gl):
    S = q_ref.shape[0]
    C = DN_CHUNK
    dk = DN_HEAD_DIM
    piece = min(S, 256)

    def conv_silu(x_ref, cw_ref, dst, l2):
        xp[0:CONV_PAD, :] = jnp.zeros((CONV_PAD, dk), F32)
        xp[CONV_PAD:CONV_PAD + S, :] = x_ref[...]
        for p in range(S // piece):
            r0 = p * piece
            acc = jnp.zeros((piece, dk), F32)
            for i in range(CONV_WIDTH):
                off = CONV_PAD + r0 - (CONV_WIDTH - 1) + i
                acc = acc + xp[off:off + piece, :] * cw_ref[i:i + 1, :]
            y = acc * jax.nn.sigmoid(acc)
            if l2:
                y = y * lax.rsqrt(jnp.sum(y * y, axis=-1, keepdims=True) + L2_EPS)
            dst[r0:r0 + piece, :] = y

    conv_silu(q_ref, cq_ref, qs, True)
    conv_silu(k_ref, ck_ref, ks, True)
    conv_silu(v_ref, cv_ref, vs, False)

    neg_a = -jnp.exp(jnp.full((1, dk), alog_ref[h], F32))
    dtb = dtb_ref[h]
    for p in range(S // piece):
        r0 = p * piece
        ba = ba_ref[r0:r0 + piece, :]
        lane = lax.broadcasted_iota(jnp.int32, ba.shape, 1)
        b_col = jnp.sum(jnp.where(lane == h, ba, 0.0), axis=-1, keepdims=True)
        a_col = jnp.sum(jnp.where(lane == DN_HEADS + h, ba, 0.0), axis=-1, keepdims=True)
        bs[r0:r0 + piece, :] = jnp.broadcast_to(jax.nn.sigmoid(b_col), (piece, dk))
        gs[r0:r0 + piece, :] = neg_a * jnp.broadcast_to(jax.nn.softplus(a_col + dtb), (piece, dk))

    ri = lax.broadcasted_iota(jnp.int32, (C, C), 0)
    ci = lax.broadcasted_iota(jnp.int32, (C, C), 1)
    incl = ri >= ci
    strict = ri > ci
    tri_ones = [jnp.where(incl, 1.0, 0.0).astype(BF16)]
    upper_ones = [jnp.where(ri <= ci, 1.0, 0.0).astype(BF16)]
    n_chunks = S // C
    group = DN_GROUP if n_chunks % DN_GROUP == 0 else 1

    def intra_load(c):
        rows = pl.ds(pl.multiple_of(c * C, C), C)
        return qs[rows, :], ks[rows, :], vs[rows, :], gs[rows, :], bs[rows, :]

    def intra_compute(loaded):
        each = lambda f, *ls: [f(*a) for a in zip(*ls)]
        q, k, v, g, beta = (list(t) for t in zip(*loaded))
        q = each(lambda t: t * (dk ** -0.5), q)
        g3 = each(lambda t: _split(t, 3), g)
        gc = each(lambda p: _mm(tri_ones, p, _EXACT_LHS), g3)
        gc_row = each(lambda p: _mm([t[:, :C] for t in p], upper_ones, _EXACT_RHS, 'tn'), g3)
        decay = each(lambda a, b: jnp.exp(jnp.where(incl, a[:, :C] - b, -jnp.inf)), gc, gc_row)
        kb = each(jnp.multiply, k, beta)
        k2 = each(lambda t: _split(t, 2), k)
        kk = each(lambda a, b: _mm(_split(a, 2), b, _X3, 'nt'), kb, k2)
        qk = each(lambda a, b: _mm(_split(a, 2), b, _X3, 'nt'), q, k2)
        m = each(lambda a, d: -jnp.where(strict, a * d, 0.0), kk, decay)
        x = each(lambda vv, bb, kbb, gg: jnp.concatenate([vv * bb, kbb * jnp.exp(gg)], axis=1),
                 v, beta, kb, gc)
        n_fac = int(math.log2(C))
        for it in range(n_fac):
            m2 = each(lambda t: _split(t, 2), m)
            x = each(lambda xx, mm: xx + _mm(mm, _split(xx, 2), _X3), x, m2)
            if it + 1 < n_fac:
                m = each(lambda mm: _mm(mm, mm, _X3), m2)
        kd2 = each(lambda kk_, gg: _split(kk_ * jnp.exp(gg[C - 1:C, :] - gg), 2), k, gc)
        pq = each(lambda a, xx: _mm(a, _split(xx, 2), _X3, 'tn'), kd2, x)
        a_intra = each(lambda a, d: jnp.where(incl, a * d, 0.0), qk, decay)
        qg = each(lambda a, gg: a * jnp.exp(gg), q, gc)
        e_last = each(lambda gg: jnp.broadcast_to(jnp.exp(gg[C - 1:C, :]), (8, dk)), gc)
        return list(zip(x, qg, a_intra, pq, e_last))

    def intra_store(c, x, qg, a_intra, pq, e_last):
        rows = pl.ds(pl.multiple_of(c * C, C), C)
        prow = pl.ds(pl.multiple_of(c * dk, dk), dk)
        us[rows, :] = x[:, :dk]
        ws[rows, :] = x[:, dk:]
        qs[rows, :] = qg
        am[rows, :] = a_intra
        qm[prow, :] = pq[:, :dk]
        pm[prow, :] = pq[:, dk:]
        gl[pl.ds(pl.multiple_of(c * 8, 8), 8), :] = e_last

    def intra_group(i, _):
        loaded = [intra_load(i * group + t) for t in range(group)]
        outs = intra_compute(loaded)
        for t in range(group):
            intra_store(i * group + t, *outs[t])
        return 0

    lax.fori_loop(0, n_chunks // group, intra_group, 0)


def _dn_scan(heads):
    dk = DN_HEAD_DIM
    n_chunks = heads[0][0].shape[0] // dk

    def scan(c, states):
        prow = pl.ds(pl.multiple_of(c * dk, dk), dk)
        ps = [_mm(_split(pm[prow, :], 2), _split(state, 2), _X3) for (pm, _, _), state in zip(heads, states)]
        for (pm, _, _), state in zip(heads, states):
            pm[prow, :] = state
        e_last = [gl[pl.ds(pl.multiple_of(c * 8, 8), 1), :] for _, _, gl in heads]
        return tuple(state * e - p + qm[prow, :]
                     for (_, qm, _), state, e, p in zip(heads, states, e_last, ps))

    lax.fori_loop(0, n_chunks, scan, tuple(jnp.zeros((dk, dk), F32) for _ in heads), unroll=2)


def _dn_back(z_ref, nw_ref, o_ref, qs, us, ws, am, pm):
    S = z_ref.shape[0]
    C = DN_CHUNK
    dk = DN_HEAD_DIM
    n_chunks = S // C
    group = DN_GROUP if n_chunks % DN_GROUP == 0 else 1
    nw = nw_ref[...]

    def out_group(i, _):
        cs = [i * group + t for t in range(group)]
        rows = [pl.ds(pl.multiple_of(c * C, C), C) for c in cs]
        each = lambda f, *ls: [f(*a) for a in zip(*ls)]
        state2 = each(lambda c: _split(pm[pl.ds(pl.multiple_of(c * dk, dk), dk), :], 2), cs)
        lhs2 = each(lambda r: _split(jnp.concatenate([ws[r, :], qs[r, :]], axis=0), 2), rows)
        prod = each(lambda a, b: _mm(a, b, _X3), lhs2, state2)
        v_new = each(lambda r, p: us[r, :] - p[:C], rows, prod)
        av = each(lambda r, vn: _mm(_split(am[r, :], 2), _split(vn, 2), _X3), rows, v_new)
        o = each(lambda p, a: p[C:] + a, prod, av)
        o = each(lambda t: t * lax.rsqrt(jnp.mean(t * t, axis=-1, keepdims=True) + DN_NORM_EPS) * nw, o)
        for r, t in zip(rows, o):
            z = z_ref[r, :]
            o_ref[r, :] = (t * (z * jax.nn.sigmoid(z))).astype(o_ref.dtype)
        return 0

    lax.fori_loop(0, n_chunks // group, out_group, 0)


DN_HEADS_PER_STEP = 2


def _dn_kernel(alog_ref, dtb_ref, q_ref, k_ref, v_ref, z_ref, ba_ref, cq_ref, ck_ref, cv_ref,
               nw_ref, o_ref, xp, *scratch):
    dk = DN_HEAD_DIM
    n_heads = o_ref.shape[1] // dk
    lanes = lambda ref, hh: ref.at[pl.ds(0, ref.shape[0]), pl.ds(hh * dk, dk)]
    views = [tuple(r.at[hh] for r in scratch) for hh in range(n_heads)]
    for hh in range(n_heads):
        _dn_front(pl.program_id(1) * n_heads + hh, alog_ref, dtb_ref, lanes(q_ref, hh), lanes(k_ref, hh),
                  lanes(v_ref, hh), ba_ref, lanes(cq_ref, hh), lanes(ck_ref, hh), lanes(cv_ref, hh),
                  xp, *views[hh])
    _dn_scan([(pm, qm, gl) for (qs, ks, vs, gs, bs, us, ws, am, pm, qm, gl) in views])
    for hh, (qs, ks, vs, gs, bs, us, ws, am, pm, qm, gl) in enumerate(views):
        _dn_back(lanes(z_ref, hh), nw_ref, lanes(o_ref, hh), qs, us, ws, am, pm)


def _deltanet(proj, conv_w, a_log, dt_bias, dn_norm_w, B, S):
    dk = DN_HEAD_DIM
    hp = DN_HEADS_PER_STEP
    seq = lambda blk: pl.BlockSpec((S, hp * dk), lambda b, h: (b, blk // hp + h))
    cw = lambda blk: pl.BlockSpec((CONV_WIDTH, hp * dk), lambda b, h: (0, blk // hp + h))
    assert all(blk % hp == 0 for blk in (DN_Q_BLK, DN_K_BLK, DN_V_BLK, DN_Z_BLK, DN_HEADS))
    return pl.pallas_call(
        _dn_kernel,
        out_shape=jax.ShapeDtypeStruct((B * S, DN_W), BF16),
        grid=(B, DN_HEADS // hp),
        in_specs=[
            pl.BlockSpec(memory_space=pltpu.SMEM),
            pl.BlockSpec(memory_space=pltpu.SMEM),
            seq(DN_Q_BLK), seq(DN_K_BLK), seq(DN_V_BLK), seq(DN_Z_BLK),
            pl.BlockSpec((S, LANES), lambda b, h: (b, DN_BA_BLK)),
            cw(0), cw(DN_HEADS), cw(2 * DN_HEADS),
            pl.BlockSpec((1, dk), lambda b, h: (0, 0)),
        ],
        out_specs=pl.BlockSpec((S, hp * dk), lambda b, h: (b, h)),
        scratch_shapes=[pltpu.VMEM((S + CONV_PAD, dk), F32)] + [pltpu.VMEM((hp, S, dk), F32)] * 7
        + [pltpu.VMEM((hp, S, DN_CHUNK), F32)] + [pltpu.VMEM((hp, S // DN_CHUNK * dk, dk), F32)] * 2
        + [pltpu.VMEM((hp, S // DN_CHUNK * 8, dk), F32)],
        compiler_params=pltpu.CompilerParams(
            dimension_semantics=("arbitrary", "arbitrary"), vmem_limit_bytes=58 * MIB),
        name="deltanet",
    )(a_log, dt_bias, proj, proj, proj, proj, proj, conv_w, conv_w, conv_w,
      dn_norm_w.reshape(1, dk))


def _pack_bf16_pairs(x):
    c = x.shape[1] // 2
    bits = lambda t: lax.bitcast_convert_type(t.astype(BF16).astype(F32), jnp.uint32)
    return (bits(x[:, :c]) >> 16) | (bits(x[:, c:]) & jnp.uint32(0xFFFF0000))


def _unpack_bf16_pairs(p):
    lo = lax.bitcast_convert_type(p << 16, F32).astype(BF16)
    hi = lax.bitcast_convert_type(p & jnp.uint32(0xFFFF0000), F32).astype(BF16)
    return lo, hi


def _outproj_kernel(a_ref, d_ref, x_ref, wa_ref, wd_ref, nw_ref, wrh_ref, wrl_ref, br_ref,
                    h_ref, u_ref, ti_ref, tg_ref, cnt_ref):
    h = x_ref[...] + _dot(a_ref[...], wa_ref[...]) + _dot(d_ref[...], wd_ref[...])
    h_ref[...] = h
    ms = jnp.mean(h * h, axis=-1, keepdims=True)
    u = h * lax.rsqrt(ms + NORM_EPS) * nw_ref[...]
    u_ref[...] = _pack_bf16_pairs(u)
    u_hi, u_lo = _split(u, 2)
    logits = (_dot(u_hi, wrh_ref[...]) + _dot(u_lo, wrh_ref[...]) + _dot(u_hi, wrl_ref[...])
              + br_ref[...])
    lane = lax.broadcasted_iota(jnp.int32, logits.shape, 1)
    lane_f = lane.astype(F32)
    vals = jnp.where(lane < N_EXPERTS, logits, -jnp.inf)
    top_v, top_i = [], []
    for _ in range(TOP_K):
        m = jnp.max(vals, axis=-1, keepdims=True)
        idx = jnp.min(jnp.where(vals == m, lane_f, float(LANES)), axis=-1, keepdims=True)
        top_v.append(m)
        top_i.append(idx)
        vals = jnp.where(lane_f == idx, -jnp.inf, vals)
    e = [jnp.exp(v - top_v[0]) for v in top_v]
    denom = e[0] + e[1] + e[2] + e[3]
    ti = jnp.zeros(logits.shape, F32)
    tg = jnp.zeros(logits.shape, F32)
    @pl.when(pl.program_id(0) == 0)
    def _():
        cnt_ref[...] = jnp.zeros(cnt_ref.shape, F32)

    tm = logits.shape[0]
    hot = [jnp.where(lane_f == idx, 1.0, 0.0) for idx in top_i]
    hot_all = hot[0] + hot[1] + hot[2] + hot[3]
    earlier = (lax.broadcasted_iota(jnp.int32, (tm, tm), 0)
               > lax.broadcasted_iota(jnp.int32, (tm, tm), 1)).astype(BF16)
    before = cnt_ref[0:1, :] + _dot(earlier, hot_all.astype(BF16))
    cnt_ref[0:1, :] = cnt_ref[0:1, :] + jnp.sum(hot_all, axis=0, keepdims=True)
    for k in range(TOP_K):
        ti = jnp.where(lane == k, top_i[k], ti)
        ti = jnp.where(lane == TOP_K + k, jnp.sum(hot[k] * before, axis=-1, keepdims=True), ti)
        tg = jnp.where(lane == k, e[k] / denom, tg)
    ti_ref[...] = ti.astype(jnp.int32)
    tg_ref[...] = tg


def _outproj_router(attn_o, dn_o, x2d, w_out, ffn_norm_w, w_router, b_router, tm):
    T, D = x2d.shape
    wa = w_out[:ATTN_Q_W].astype(BF16)
    wd = w_out[ATTN_Q_W:].astype(BF16)
    wr = jnp.zeros((D, LANES), F32).at[:, :N_EXPERTS].set(w_router)
    wr_hi = wr.astype(BF16)
    wr_lo = (wr - wr_hi.astype(F32)).astype(BF16)
    br = jnp.zeros((1, LANES), F32).at[0, :N_EXPERTS].set(b_router)
    const = lambda shape: pl.BlockSpec(shape, lambda i: (0, 0))
    tile = lambda w: pl.BlockSpec((tm, w), lambda i: (i, 0))
    return pl.pallas_call(
        _outproj_kernel,
        out_shape=(jax.ShapeDtypeStruct((T, D), F32), jax.ShapeDtypeStruct((T, D // 2), jnp.uint32),
                   jax.ShapeDtypeStruct((T, LANES), jnp.int32),
                   jax.ShapeDtypeStruct((T, LANES), F32), jax.ShapeDtypeStruct((8, LANES), F32)),
        grid=(T // tm,),
        in_specs=[tile(ATTN_Q_W), tile(DN_W), tile(D), const((ATTN_Q_W, D)), const((DN_W, D)),
                  const((1, D)), const((D, LANES)), const((D, LANES)), const((1, LANES))],
        out_specs=(tile(D), tile(D // 2), tile(LANES), tile(LANES), const((8, LANES))),
        compiler_params=pltpu.CompilerParams(
            dimension_semantics=("arbitrary",), vmem_limit_bytes=56 * MIB),
        name="outproj",
    )(attn_o, dn_o, x2d, wa, wd, ffn_norm_w.reshape(1, D), wr_hi, wr_lo, br)


def _experts_kernel(ie_ref, ib_ref, in_ref, used_ref, x_ref, wg_ref, wl_ref, wd_ref,
                    bg_ref, bl_ref, bd_ref, y_ref, xraw, ybuf, wg_s, wl_s, wd_s, sem_x, sem_y):
    w = pl.program_id(0)
    j = pl.program_id(1)
    nw = pl.num_programs(0)
    nj = pl.num_programs(1)
    R = MOE_ROWS
    nb = in_ref[w]
    row0 = ib_ref[w] * R
    slot = lax.rem(w, 2)
    w_next = jnp.minimum(w + 1, nw - 1)
    nb_next = jnp.where(w + 1 < nw, in_ref[w_next], 0)
    row0_next = ib_ref[w_next] * R

    def x_copy(base, s, r):
        return pltpu.make_async_copy(x_ref.at[pl.ds(base + r * R, R), :],
                                     xraw.at[s, pl.ds(r * R, R), :], sem_x.at[s])

    def y_copy(r):
        return pltpu.make_async_copy(ybuf.at[pl.ds(r * R, R), :],
                                     y_ref.at[pl.ds(row0 + r * R, R), :], sem_y)

    def for_range(n, fn):
        def body(r, _):
            fn(r)
            return 0
        lax.fori_loop(0, n, body, 0)

    def rows_of(r):
        return pl.ds(pl.multiple_of(r * R, R), R)

    @pl.when((j == 0) & (w == 0))
    def _():
        for_range(nb, lambda r: x_copy(row0, slot, r).start())

    @pl.when(j == 0)
    def _():
        bd = jnp.broadcast_to(bd_ref[...], (R, ybuf.shape[1]))

        def init(r):
            ybuf[rows_of(r), :] = bd

        for_range(nb, init)
        for_range(nb, lambda r: x_copy(row0, slot, r).wait())
        for_range(nb_next, lambda r: x_copy(row0_next, 1 - slot, r).start())

    @pl.when(nb > 0)
    def _():
        wg_s[...] = wg_ref[...].astype(BF16)
        wl_s[...] = wl_ref[...].astype(BF16)
        wd_s[...] = wd_ref[...].astype(BF16)

    bg = bg_ref[...]
    bl = bl_ref[...]

    def blocks(rs, write_back):
        rows = [rows_of(r) for r in rs]
        xs = [jnp.concatenate(_unpack_bf16_pairs(xraw[slot, rw, :]), axis=1) for rw in rows]
        hg = [jnp.minimum(_dot(x, wg_s[...]) + bg, SWIGLU_LIMIT) for x in xs]
        hl = [jnp.clip(_dot(x, wl_s[...]) + bl, -SWIGLU_LIMIT, SWIGLU_LIMIT) for x in xs]
        acts = [(a * jax.nn.sigmoid(SWIGLU_ALPHA * a) * (b + 1.0)).astype(BF16) for a, b in zip(hg, hl)]
        for rw, act in zip(rows, acts):
            ybuf[rw, :] += _dot(act, wd_s[...])
        if write_back:
            for r in rs:
                y_copy(r).start()

    def all_blocks(write_back):
        lock = MOE_LOCKSTEP_BLOCKS

        def full(i, _):
            blocks([lock * i + t for t in range(lock)], write_back)
            return 0

        lax.fori_loop(0, nb // lock, full, 0)
        base = (nb // lock) * lock
        size = lock // 2
        while size >= 1:
            @pl.when(((nb - base) & size) != 0)
            def _(start=base, size=size):
                blocks([start + t for t in range(size)], write_back)

            base = base + ((nb - base) & size)
            size //= 2

    @pl.when(j < nj - 1)
    def _():
        all_blocks(False)

    @pl.when(j == nj - 1)
    def _():
        all_blocks(True)
        for_range(nb, lambda r: y_copy(r).wait())

    @pl.when((j == nj - 1) & (w == pl.num_programs(0) - 1))
    def _():
        ybuf[0:R, :] = jnp.zeros((R, ybuf.shape[1]), F32)

        def tail_copy(r):
            return pltpu.make_async_copy(ybuf.at[pl.ds(0, R), :], y_ref.at[pl.ds(r * R, R), :], sem_y)

        def tail(fn):
            def body(r, _):
                fn(r)
                return 0
            lax.fori_loop(used_ref[0], y_ref.shape[0] // R, body, 0)

        tail(lambda r: tail_copy(r).start())
        tail(lambda r: tail_copy(r).wait())


def _experts(x_sorted, item_expert, item_blk0, item_nblk, n_blocks_used,
             w_gate_up, b_gate_up, w_down, b_down):
    P = x_sorted.shape[0]
    D = x_sorted.shape[1] * 2
    E, _, F2 = w_gate_up.shape
    F = F2 // 2
    tf = MOE_F_TILE
    nj = F // tf
    n_items = item_expert.shape[0]
    rows = MOE_ITEM_BLOCKS * MOE_ROWS
    b_gu = b_gate_up.reshape(E, 1, F2)
    b_d = b_down.reshape(E, 1, D)
    jt = lambda j, n: jnp.where(n > 0, j, nj - 1)
    return pl.pallas_call(
        _experts_kernel,
        out_shape=jax.ShapeDtypeStruct((P, D), F32),
        grid_spec=pltpu.PrefetchScalarGridSpec(
            num_scalar_prefetch=4,
            grid=(n_items, nj),
            in_specs=[
                pl.BlockSpec(memory_space=pl.ANY),
                pl.BlockSpec((None, D, tf), lambda w, j, ie, ib, nb, *_: (ie[w], 0, jt(j, nb[w]))),
                pl.BlockSpec((None, D, tf), lambda w, j, ie, ib, nb, *_: (ie[w], 0, nj + jt(j, nb[w]))),
                pl.BlockSpec((None, tf, D), lambda w, j, ie, ib, nb, *_: (ie[w], jt(j, nb[w]), 0)),
                pl.BlockSpec((None, 1, tf), lambda w, j, ie, ib, nb, *_: (ie[w], 0, jt(j, nb[w]))),
                pl.BlockSpec((None, 1, tf), lambda w, j, ie, ib, nb, *_: (ie[w], 0, nj + jt(j, nb[w]))),
                pl.BlockSpec((None, 1, D), lambda w, j, ie, *_: (ie[w], 0, 0)),
            ],
            out_specs=pl.BlockSpec(memory_space=pl.ANY),
            scratch_shapes=[
                pltpu.VMEM((2, rows, D // 2), jnp.uint32), pltpu.VMEM((rows, D), F32),
                pltpu.VMEM((D, tf), BF16), pltpu.VMEM((D, tf), BF16), pltpu.VMEM((tf, D), BF16),
                pltpu.SemaphoreType.DMA((2,)), pltpu.SemaphoreType.DMA,
            ],
        ),
        compiler_params=pltpu.CompilerParams(
            dimension_semantics=("arbitrary", "arbitrary"), vmem_limit_bytes=60 * MIB),
        name="experts",
    )(item_expert, item_blk0, item_nblk, n_blocks_used, x_sorted,
      w_gate_up, w_gate_up, w_down, b_gu, b_gu, b_d)


def _combine_kernel(dest_ref, h_ref, tg_ref, y_ref, o_ref, buf, sem):
    i = pl.program_id(0)
    n = pl.num_programs(0)
    tm = h_ref.shape[0]

    def issue_tile(step):
        slot = lax.rem(step, 2)

        def issue(r, _):
            for t in range(ROWS_PER_ISSUE):
                for k in range(TOP_K):
                    row = dest_ref[(step * tm + ROWS_PER_ISSUE * r + t) * TOP_K + k]
                    pltpu.make_async_copy(y_ref.at[pl.ds(row, 1), :],
                                          buf.at[slot, k, pl.ds(ROWS_PER_ISSUE * r + t, 1), :],
                                          sem.at[slot]).start()
            return 0

        lax.fori_loop(0, tm // ROWS_PER_ISSUE, issue, 0)

    @pl.when(i == 0)
    def _():
        issue_tile(0)

    @pl.when(i + 1 < n)
    def _():
        issue_tile(i + 1)

    slot = lax.rem(i, 2)
    for k in range(TOP_K):
        pltpu.make_async_copy(y_ref.at[pl.ds(0, tm), :], buf.at[slot, k], sem.at[slot]).wait()
    tg = tg_ref[...]
    acc = h_ref[...]
    for k in range(TOP_K):
        acc = acc + tg[:, k:k + 1] * buf[slot, k]
    o_ref[...] = acc


def _combine(h, top_g, y_sorted, dest, tm):
    T, D = h.shape
    return pl.pallas_call(
        _combine_kernel,
        out_shape=jax.ShapeDtypeStruct((T, D), F32),
        grid_spec=pltpu.PrefetchScalarGridSpec(
            num_scalar_prefetch=1,
            grid=(T // tm,),
            in_specs=[
                pl.BlockSpec((tm, D), lambda i, d: (i, 0)),
                pl.BlockSpec((tm, LANES), lambda i, d: (i, 0)),
                pl.BlockSpec(memory_space=pl.ANY),
            ],
            out_specs=pl.BlockSpec((tm, D), lambda i, d: (i, 0)),
            scratch_shapes=[pltpu.VMEM((2, TOP_K, tm, D), F32), pltpu.SemaphoreType.DMA((2,))],
        ),
        compiler_params=pltpu.CompilerParams(dimension_semantics=("arbitrary",)),
        name="combine",
    )(dest, h, top_g, y_sorted)


def _dispatch_kernel(dest_ref, row_end_ref, blk_end_ref, u_ref, x_ref, buf, zbuf, sem_l, sem_s, sem_z):
    i = pl.program_id(0)
    n = pl.num_programs(0)
    tm = buf.shape[1]
    R = MOE_ROWS
    n_blocks = x_ref.shape[0] // R

    def load(step):
        s = lax.rem(step, 2)
        return pltpu.make_async_copy(u_ref.at[pl.ds(step * tm, tm), :], buf.at[s], sem_l.at[s])

    def zero_copy(blk):
        return pltpu.make_async_copy(zbuf, x_ref.at[pl.ds(blk * R, R), :], sem_z)

    def wait_scatter(step):
        s = lax.rem(step, 2)
        for _ in range(TOP_K):
            pltpu.make_async_copy(u_ref.at[pl.ds(0, tm), :], buf.at[s], sem_s.at[s]).wait()

    @pl.when(i == 0)
    def _():
        zbuf[...] = jnp.zeros(zbuf.shape, zbuf.dtype)

        def partial_blocks(fn):
            def body(e, _):
                @pl.when(lax.rem(row_end_ref[e], R) != 0)
                def _():
                    fn(blk_end_ref[e] - 1)
                return 0
            lax.fori_loop(0, N_EXPERTS, body, 0)

        def unused_blocks(fn):
            def body(b, _):
                fn(b)
                return 0
            lax.fori_loop(blk_end_ref[N_EXPERTS - 1], n_blocks, body, 0)

        partial_blocks(lambda b: zero_copy(b).start())
        unused_blocks(lambda b: zero_copy(b).start())
        partial_blocks(lambda b: zero_copy(b).wait())
        unused_blocks(lambda b: zero_copy(b).wait())
        load(0).start()

    @pl.when(i > 0)
    def _():
        wait_scatter(i - 1)

    @pl.when(i + 1 < n)
    def _():
        load(i + 1).start()

    load(i).wait()
    slot = lax.rem(i, 2)

    def issue(r, _):
        for t in range(ROWS_PER_ISSUE):
            for k in range(TOP_K):
                row = dest_ref[(i * tm + ROWS_PER_ISSUE * r + t) * TOP_K + k]
                pltpu.make_async_copy(buf.at[slot, pl.ds(ROWS_PER_ISSUE * r + t, 1), :],
                                      x_ref.at[pl.ds(row, 1), :], sem_s.at[slot]).start()
        return 0

    lax.fori_loop(0, tm // ROWS_PER_ISSUE, issue, 0)

    @pl.when(i == n - 1)
    def _():
        wait_scatter(i)


def _dispatch(u_packed, dest, row_end, blk_end, P, tm):
    T, W = u_packed.shape
    return pl.pallas_call(
        _dispatch_kernel,
        out_shape=jax.ShapeDtypeStruct((P, W), u_packed.dtype),
        grid_spec=pltpu.PrefetchScalarGridSpec(
            num_scalar_prefetch=3,
            grid=(T // tm,),
            in_specs=[pl.BlockSpec(memory_space=pl.ANY)],
            out_specs=pl.BlockSpec(memory_space=pl.ANY),
            scratch_shapes=[pltpu.VMEM((2, tm, W), u_packed.dtype), pltpu.VMEM((MOE_ROWS, W), u_packed.dtype),
                            pltpu.SemaphoreType.DMA((2,)), pltpu.SemaphoreType.DMA((2,)),
                            pltpu.SemaphoreType.DMA],
        ),
        compiler_params=pltpu.CompilerParams(dimension_semantics=("arbitrary",)),
        name="dispatch",
    )(dest, row_end, blk_end, u_packed)


def _routing(top_i, expert_counts, T):
    R = MOE_ROWS
    n_assign = T * TOP_K
    e_flat = top_i[:, :TOP_K].reshape(n_assign)
    rank = top_i[:, TOP_K:2 * TOP_K].reshape(n_assign)
    counts = expert_counts[0, :N_EXPERTS].astype(jnp.int32)
    nblk = (counts + R - 1) // R
    blk_end = jnp.cumsum(nblk)
    blk_start = blk_end - nblk
    dest = blk_start[e_flat] * R + rank
    max_blocks = n_assign // R + N_EXPERTS
    P = max_blocks * R
    dest = dest.astype(jnp.int32)
    row_end = (blk_start * R + counts).astype(jnp.int32)
    nb_item = MOE_ITEM_BLOCKS
    n_items_e = (nblk + nb_item - 1) // nb_item
    item_end = jnp.cumsum(n_items_e)
    item_start = item_end - n_items_e
    max_items = (max_blocks + (nb_item - 1) * N_EXPERTS) // nb_item
    w = jnp.arange(max_items, dtype=jnp.int32)
    ie = jnp.minimum(jnp.searchsorted(item_end, w, side='right'), N_EXPERTS - 1).astype(jnp.int32)
    local = w - item_start[ie]
    live = w < item_end[-1]
    ib = blk_start[ie] + local * nb_item
    inb = jnp.clip(nblk[ie] - local * nb_item, 0, nb_item)
    last_e = ie[jnp.maximum(item_end[-1] - 1, 0)]
    ie = jnp.where(live, ie, last_e).astype(jnp.int32)
    ib = jnp.where(live, ib, 0).astype(jnp.int32)
    inb = jnp.where(live, inb, 0).astype(jnp.int32)
    return dest, row_end, blk_end.astype(jnp.int32), P, ie, ib, inb


def _layer(x, attn_norm_w, w_in, q_norm_w, k_norm_w, attn_sinks, conv_w, a_log, dt_bias,
           dn_norm_w, w_out, ffn_norm_w, w_router, b_router, w_gate_up, b_gate_up, w_down, b_down):
    B, S, D = x.shape
    T = B * S
    x2d = x.reshape(T, D)
    proj = _inproj(x2d, attn_norm_w, w_in, tm=min(T, 1024), tn=1024)
    attn_o = _attention(proj, attn_sinks, q_norm_w, k_norm_w, B, S)
    dn_o = _deltanet(proj, conv_w, a_log, dt_bias, dn_norm_w, B, S)
    h, u, top_i, top_g, expert_counts = _outproj_router(attn_o, dn_o, x2d, w_out, ffn_norm_w, w_router,
                                                        b_router, tm=min(T, 512))
    dest, row_end, blk_end, P, ie, ib, inb = _routing(top_i, expert_counts, T)
    x_sorted = _dispatch(u, dest, row_end, blk_end, P, tm=min(T, 256))
    y_sorted = _experts(x_sorted, ie, ib, inb, blk_end[-1:], w_gate_up, b_gate_up, w_down, b_down)
    out = _combine(h, top_g, y_sorted, dest, tm=min(T, 128))
    return out.reshape(B, S, D)


def kernel(x, attn_norm_w, w_in, q_norm_w, k_norm_w, attn_sinks, conv_w, a_log, dt_bias, dn_norm_w,
           w_out, ffn_norm_w, w_router, b_router, w_gate_up, b_gate_up, w_down, b_down):
    h = x
    for l in range(attn_norm_w.shape[0]):
        h = _layer(h, attn_norm_w[l], w_in[l], q_norm_w[l], k_norm_w[l], attn_sinks[l], conv_w[l],
                   a_log[l], dt_bias[l], dn_norm_w[l], w_out[l], ffn_norm_w[l], w_router[l],
                   b_router[l], w_gate_up[l], b_gate_up[l], w_down[l], b_down[l])
    return h
```

```python
import functools
import math

import jax
import jax.numpy as jnp
from jax import lax
from jax.experimental import pallas as pl
from jax.experimental.pallas import tpu as pltpu

F32 = jnp.float32
BF16 = jnp.bfloat16

D_MODEL = 2048
ATTN_HEAD_DIM = 64
ATTN_HEADS = 16
ATTN_KV_HEADS = 4
ATTN_GROUP = ATTN_HEADS // ATTN_KV_HEADS
WINDOW = 128
DN_HEAD_DIM = 128
DN_HEADS = 8
DN_CHUNK = 64
CONV_WIDTH = 4
ATTN_Q_W = ATTN_HEADS * ATTN_HEAD_DIM
ATTN_KV_W = ATTN_KV_HEADS * ATTN_HEAD_DIM
DN_W = DN_HEADS * DN_HEAD_DIM
IN_WIDTH = ATTN_Q_W + 2 * ATTN_KV_W + 4 * DN_W + 2 * DN_HEADS
N_EXPERTS = 32
TOP_K = 4
D_EXPERT = 2048
SWIGLU_ALPHA = 1.702
SWIGLU_LIMIT = 7.0
NORM_EPS = 1e-5
QK_NORM_EPS = 1e-6
DN_NORM_EPS = 1e-6
L2_EPS = 1e-6

LANES = 128
MIB = 1024 * 1024

ATTN_K_BLK = ATTN_Q_W // ATTN_KV_W
ATTN_V_BLK = ATTN_K_BLK + 1
DN_Q_BLK = (ATTN_Q_W + 2 * ATTN_KV_W) // DN_HEAD_DIM
DN_K_BLK = DN_Q_BLK + DN_HEADS
DN_V_BLK = DN_K_BLK + DN_HEADS
DN_Z_BLK = DN_V_BLK + DN_HEADS
DN_BA_BLK = DN_Z_BLK + DN_HEADS

MOE_ROWS = 256
MOE_ITEM_BLOCKS = 9
MOE_F_TILE = 256
MOE_LOCKSTEP_BLOCKS = 4
ROWS_PER_ISSUE = 4


def _dot(a, b):
    return jnp.dot(a, b, preferred_element_type=F32)


def _dot_nt(a, b):
    return lax.dot_general(a, b, (((1,), (1,)), ((), ())), preferred_element_type=F32)


def _dot_tn(a, b):
    return lax.dot_general(a, b, (((0,), (0,)), ((), ())), preferred_element_type=F32)


def _inproj_kernel(x_ref, nw_ref, w_ref, o_ref, u_ref, *, last_width):
    @pl.when(pl.program_id(1) == 0)
    def _():
        x = x_ref[...]
        ms = jnp.mean(x * x, axis=-1, keepdims=True)
        u_ref[...] = (x * lax.rsqrt(ms + NORM_EPS) * nw_ref[...]).astype(BF16)

    last = pl.num_programs(1) - 1

    @pl.when(pl.program_id(1) < last)
    def _():
        o_ref[...] = _dot(u_ref[...], w_ref[...])

    @pl.when(pl.program_id(1) == last)
    def _():
        o_ref[:, :last_width] = _dot(u_ref[...], w_ref[:, :last_width])


def _inproj(x2d, norm_w, w_in, tm, tn):
    T, D = x2d.shape
    N = w_in.shape[1]
    n_col_blocks = pl.cdiv(N, tn)
    last_width = min(tn, -(-(N - (n_col_blocks - 1) * tn) // LANES) * LANES)
    return pl.pallas_call(
        functools.partial(_inproj_kernel, last_width=last_width),
        out_shape=jax.ShapeDtypeStruct((T, N), F32),
        grid=(T // tm, pl.cdiv(N, tn)),
        in_specs=[
            pl.BlockSpec((tm, D), lambda i, j: (i, 0)),
            pl.BlockSpec((1, D), lambda i, j: (0, 0)),
            pl.BlockSpec((D, tn), lambda i, j: (0, j)),
        ],
        out_specs=pl.BlockSpec((tm, tn), lambda i, j: (i, j)),
        scratch_shapes=[pltpu.VMEM((tm, D), BF16)],
        compiler_params=pltpu.CompilerParams(
            dimension_semantics=("arbitrary", "arbitrary"), vmem_limit_bytes=56 * MIB),
        name="inproj",
    )(x2d, norm_w.reshape(1, D), w_in.astype(BF16))


def _attn_kernel(sink_ref, q_ref, kp_ref, kc_ref, vp_ref, vc_ref, qw_ref, kw_ref, o_ref):
    n = pl.program_id(1)
    L = WINDOW
    d = ATTN_HEAD_DIM

    def head_norm(t, w):
        ms = jnp.mean(t * t, axis=-1, keepdims=True)
        return t * lax.rsqrt(ms + QK_NORM_EPS) * w

    G = ATTN_GROUP
    qi = lax.broadcasted_iota(jnp.int32, (L, 2 * L), 0)
    kj = lax.broadcasted_iota(jnp.int32, (L, 2 * L), 1)
    dist = qi + L - kj
    valid = (dist >= 0) & (dist < WINDOW) & ((kj >= L) | (n > 0))
    dist_f = dist.astype(F32)

    qw = qw_ref[...]
    kw = kw_ref[...]
    kvs = range(ATTN_KV_HEADS)
    heads = lambda kv: [kv * G + g for g in range(G)]
    cols = lambda h: slice(h * d, (h + 1) * d)
    k = [head_norm(jnp.concatenate([kp_ref[:, cols(kv)], kc_ref[:, cols(kv)]], axis=0), kw).astype(BF16)
         for kv in kvs]
    v = [jnp.concatenate([vp_ref[:, cols(kv)], vc_ref[:, cols(kv)]], axis=0).astype(BF16) for kv in kvs]
    q = [head_norm(jnp.concatenate([q_ref[:, cols(h)] for h in heads(kv)], axis=0), qw).astype(BF16)
         for kv in kvs]
    s4 = [_dot_nt(q[kv], k[kv]) for kv in kvs]
    hs = range(ATTN_HEADS)
    s = [s4[h // G][(h % G) * L:(h % G + 1) * L, :] for h in hs]
    s = [jnp.where(valid, s[h] * (d ** -0.5) - 2.0 ** (-8.0 * (h + 1) / ATTN_HEADS) * dist_f, -jnp.inf)
         for h in hs]
    m = [jnp.maximum(jnp.max(s[h], axis=-1, keepdims=True), sink_ref[h]) for h in hs]
    p = [jnp.exp(s[h] - m[h]) for h in hs]
    denom = [jnp.sum(p[h], axis=-1, keepdims=True) + jnp.exp(sink_ref[h] - m[h]) for h in hs]
    probs = [(p[h] / denom[h]).astype(BF16) for h in hs]
    o = [_dot(jnp.concatenate([probs[h] for h in heads(kv)], axis=0), v[kv]) for kv in kvs]
    for h in hs:
        o_ref[:, cols(h)] = o[h // G][(h % G) * L:(h % G + 1) * L, :].astype(o_ref.dtype)


def _attention(proj, sinks, q_norm_w, k_norm_w, B, S):
    nb = S // WINDOW
    L = WINDOW
    row = lambda b, n: b * nb + n
    prev = lambda b, n: b * nb + jnp.maximum(n - 1, 0)
    return pl.pallas_call(
        _attn_kernel,
        out_shape=jax.ShapeDtypeStruct((B * S, ATTN_Q_W), BF16),
        grid=(B, nb),
        in_specs=[
            pl.BlockSpec(memory_space=pltpu.SMEM),
            pl.BlockSpec((L, ATTN_Q_W), lambda b, n: (row(b, n), 0)),
            pl.BlockSpec((L, ATTN_KV_W), lambda b, n: (prev(b, n), ATTN_K_BLK)),
            pl.BlockSpec((L, ATTN_KV_W), lambda b, n: (row(b, n), ATTN_K_BLK)),
            pl.BlockSpec((L, ATTN_KV_W), lambda b, n: (prev(b, n), ATTN_V_BLK)),
            pl.BlockSpec((L, ATTN_KV_W), lambda b, n: (row(b, n), ATTN_V_BLK)),
            pl.BlockSpec((1, ATTN_HEAD_DIM), lambda b, n: (0, 0)),
            pl.BlockSpec((1, ATTN_HEAD_DIM), lambda b, n: (0, 0)),
        ],
        out_specs=pl.BlockSpec((L, ATTN_Q_W), lambda b, n: (row(b, n), 0)),
        compiler_params=pltpu.CompilerParams(dimension_semantics=("arbitrary", "arbitrary")),
        name="attn",
    )(sinks, proj, proj, proj, proj, proj, q_norm_w.reshape(1, -1), k_norm_w.reshape(1, -1))


CONV_PAD = 8


DN_GROUP = 16

_X3 = ((0, 0), (1, 0), (0, 1))
_EXACT_LHS = ((0, 0), (0, 1), (0, 2))
_EXACT_RHS = ((0, 0), (1, 0), (2, 0))


def _split(x, n):
    parts = []
    for i in range(n):
        p = x.astype(BF16)
        parts.append(p)
        if i + 1 < n:
            x = x - p.astype(F32)
    return parts


def _mm(a_parts, b_parts, terms, form='nn'):
    a_axis = 0 if form == 'tn' else 1
    b_axis = 1 if form == 'nt' else 0
    a = jnp.concatenate([a_parts[i] for i, _ in terms], axis=a_axis)
    b = jnp.concatenate([b_parts[j] for _, j in terms], axis=b_axis)
    return {'nn': _dot, 'nt': _dot_nt, 'tn': _dot_tn}[form](a, b)


def _dn_front(h, alog_ref, dtb_ref, q_ref, k_ref, v_ref, ba_ref, cq_ref, ck_ref, cv_ref,
              xp, qs, ks, vs, gs, bs, us, ws, am, pm, qm, gl):
    S = q_ref.shape[0]
    C = DN_CHUNK
    dk = DN_HEAD_DIM
    piece = min(S, 256)

    def conv_silu(x_ref, cw_ref, dst, l2):
        xp[0:CONV_PAD, :] = jnp.zeros((CONV_PAD, dk), F32)
        xp[CONV_PAD:CONV_PAD + S, :] = x_ref[...]
        for p in range(S // piece):
            r0 = p * piece
            acc = jnp.zeros((piece, dk), F32)
            for i in range(CONV_WIDTH):
                off = CONV_PAD + r0 - (CONV_WIDTH - 1) + i
                acc = acc + xp[off:off + piece, :] * cw_ref[i:i + 1, :]
            y = acc * jax.nn.sigmoid(acc)
            if l2:
                y = y * lax.rsqrt(jnp.sum(y * y, axis=-1, keepdims=True) + L2_EPS)
            dst[r0:r0 + piece, :] = y

    conv_silu(q_ref, cq_ref, qs, True)
    conv_silu(k_ref, ck_ref, ks, True)
    conv_silu(v_ref, cv_ref, vs, False)

    neg_a = -jnp.exp(jnp.full((1, dk), alog_ref[h], F32))
    dtb = dtb_ref[h]
    for p in range(S // piece):
        r0 = p * piece
        ba = ba_ref[r0:r0 + piece, :]
        lane = lax.broadcasted_iota(jnp.int32, ba.shape, 1)
        b_col = jnp.sum(jnp.where(lane == h, ba, 0.0), axis=-1, keepdims=True)
        a_col = jnp.sum(jnp.where(lane == DN_HEADS + h, ba, 0.0), axis=-1, keepdims=True)
        bs[r0:r0 + piece, :] = jnp.broadcast_to(jax.nn.sigmoid(b_col), (piece, dk))
        gs[r0:r0 + piece, :] = neg_a * jnp.broadcast_to(jax.nn.softplus(a_col + dtb), (piece, dk))

    ri = lax.broadcasted_iota(jnp.int32, (C, C), 0)
    ci = lax.broadcasted_iota(jnp.int32, (C, C), 1)
    incl = ri >= ci
    strict = ri > ci
    tri_ones = [jnp.where(incl, 1.0, 0.0).astype(BF16)]
    upper_ones = [jnp.where(ri <= ci, 1.0, 0.0).astype(BF16)]
    n_chunks = S // C
    group = DN_GROUP if n_chunks % DN_GROUP == 0 else 1

    def intra_load(c):
        rows = pl.ds(pl.multiple_of(c * C, C), C)
        return qs[rows, :], ks[rows, :], vs[rows, :], gs[rows, :], bs[rows, :]

    def intra_compute(loaded):
        each = lambda f, *ls: [f(*a) for a in zip(*ls)]
        q, k, v, g, beta = (list(t) for t in zip(*loaded))
        q = each(lambda t: t * (dk ** -0.5), q)
        g3 = each(lambda t: _split(t, 3), g)
        gc = each(lambda p: _mm(tri_ones, p, _EXACT_LHS), g3)
        gc_row = each(lambda p: _mm([t[:, :C] for t in p], upper_ones, _EXACT_RHS, 'tn'), g3)
        decay = each(lambda a, b: jnp.exp(jnp.where(incl, a[:, :C] - b, -jnp.inf)), gc, gc_row)
        kb = each(jnp.multiply, k, beta)
        k2 = each(lambda t: _split(t, 2), k)
        kk = each(lambda a, b: _mm(_split(a, 2), b, _X3, 'nt'), kb, k2)
        qk = each(lambda a, b: _mm(_split(a, 2), b, _X3, 'nt'), q, k2)
        m = each(lambda a, d: -jnp.where(strict, a * d, 0.0), kk, decay)
        x = each(lambda vv, bb, kbb, gg: jnp.concatenate([vv * bb, kbb * jnp.exp(gg)], axis=1),
                 v, beta, kb, gc)
        n_fac = int(math.log2(C))
        for it in range(n_fac):
            m2 = each(lambda t: _split(t, 2), m)
            x = each(lambda xx, mm: xx + _mm(mm, _split(xx, 2), _X3), x, m2)
            if it + 1 < n_fac:
                m = each(lambda mm: _mm(mm, mm, _X3), m2)
        kd2 = each(lambda kk_, gg: _split(kk_ * jnp.exp(gg[C - 1:C, :] - gg), 2), k, gc)
        pq = each(lambda a, xx: _mm(a, _split(xx, 2), _X3, 'tn'), kd2, x)
        a_intra = each(lambda a, d: jnp.where(incl, a * d, 0.0), qk, decay)
        qg = each(lambda a, gg: a * jnp.exp(gg), q, gc)
        e_last = each(lambda gg: jnp.broadcast_to(jnp.exp(gg[C - 1:C, :]), (8, dk)), gc)
        return list(zip(x, qg, a_intra, pq, e_last))

    def intra_store(c, x, qg, a_intra, pq, e_last):
        rows = pl.ds(pl.multiple_of(c * C, C), C)
        prow = pl.ds(pl.multiple_of(c * dk, dk), dk)
        us[rows, :] = x[:, :dk]
        ws[rows, :] = x[:, dk:]
        qs[rows, :] = qg
        am[rows, :] = a_intra
        qm[prow, :] = pq[:, :dk]
        pm[prow, :] = pq[:, dk:]
        gl[pl.ds(pl.multiple_of(c * 8, 8), 8), :] = e_last

    def intra_group(i, _):
        loaded = [intra_load(i * group + t) for t in range(group)]
        outs = intra_compute(loaded)
        for t in range(group):
            intra_store(i * group + t, *outs[t])
        return 0

    lax.fori_loop(0, n_chunks // group, intra_group, 0)


def _dn_scan(heads):
    dk = DN_HEAD_DIM
    n_chunks = heads[0][0].shape[0] // dk

    def scan(c, states):
        prow = pl.ds(pl.multiple_of(c * dk, dk), dk)
        ps = [_mm(_split(pm[prow, :], 2), _split(state, 2), _X3) for (pm, _, _), state in zip(heads, states)]
        for (pm, _, _), state in zip(heads, states):
            pm[prow, :] = state
        e_last = [gl[pl.ds(pl.multiple_of(c * 8, 8), 1), :] for _, _, gl in heads]
        return tuple(state * e - p + qm[prow, :]
                     for (_, qm, _), state, e, p in zip(heads, states, e_last, ps))

    lax.fori_loop(0, n_chunks, scan, tuple(jnp.zeros((dk, dk), F32) for _ in heads), unroll=2)


def _dn_back(z_ref, nw_ref, o_ref, qs, us, ws, am, pm):
    S = z_ref.shape[0]
    C = DN_CHUNK
    dk = DN_HEAD_DIM
    n_chunks = S // C
    group = DN_GROUP if n_chunks % DN_GROUP == 0 else 1
    nw = nw_ref[...]

    def out_group(i, _):
        cs = [i * group + t for t in range(group)]
        rows = [pl.ds(pl.multiple_of(c * C, C), C) for c in cs]
        each = lambda f, *ls: [f(*a) for a in zip(*ls)]
        state2 = each(lambda c: _split(pm[pl.ds(pl.multiple_of(c * dk, dk), dk), :], 2), cs)
        lhs2 = each(lambda r: _split(jnp.concatenate([ws[r, :], qs[r, :]], axis=0), 2), rows)
        prod = each(lambda a, b: _mm(a, b, _X3), lhs2, state2)
        v_new = each(lambda r, p: us[r, :] - p[:C], rows, prod)
        av = each(lambda r, vn: _mm(_split(am[r, :], 2), _split(vn, 2), _X3), rows, v_new)
        o = each(lambda p, a: p[C:] + a, prod, av)
        o = each(lambda t: t * lax.rsqrt(jnp.mean(t * t, axis=-1, keepdims=True) + DN_NORM_EPS) * nw, o)
        for r, t in zip(rows, o):
            z = z_ref[r, :]
            o_ref[r, :] = (t * (z * jax.nn.sigmoid(z))).astype(o_ref.dtype)
        return 0

    lax.fori_loop(0, n_chunks // group, out_group, 0)


DN_HEADS_PER_STEP = 2


def _dn_kernel(alog_ref, dtb_ref, q_ref, k_ref, v_ref, z_ref, ba_ref, cq_ref, ck_ref, cv_ref,
               nw_ref, o_ref, xp, *scratch):
    dk = DN_HEAD_DIM
    n_heads = o_ref.shape[1] // dk
    lanes = lambda ref, hh: ref.at[pl.ds(0, ref.shape[0]), pl.ds(hh * dk, dk)]
    views = [tuple(r.at[hh] for r in scratch) for hh in range(n_heads)]
    for hh in range(n_heads):
        _dn_front(pl.program_id(1) * n_heads + hh, alog_ref, dtb_ref, lanes(q_ref, hh), lanes(k_ref, hh),
                  lanes(v_ref, hh), ba_ref, lanes(cq_ref, hh), lanes(ck_ref, hh), lanes(cv_ref, hh),
                  xp, *views[hh])
    _dn_scan([(pm, qm, gl) for (qs, ks, vs, gs, bs, us, ws, am, pm, qm, gl) in views])
    for hh, (qs, ks, vs, gs, bs, us, ws, am, pm, qm, gl) in enumerate(views):
        _dn_back(lanes(z_ref, hh), nw_ref, lanes(o_ref, hh), qs, us, ws, am, pm)


def _deltanet(proj, conv_w, a_log, dt_bias, dn_norm_w, B, S):
    dk = DN_HEAD_DIM
    hp = DN_HEADS_PER_STEP
    seq = lambda blk: pl.BlockSpec((S, hp * dk), lambda b, h: (b, blk // hp + h))
    cw = lambda blk: pl.BlockSpec((CONV_WIDTH, hp * dk), lambda b, h: (0, blk // hp + h))
    assert all(blk % hp == 0 for blk in (DN_Q_BLK, DN_K_BLK, DN_V_BLK, DN_Z_BLK, DN_HEADS))
    return pl.pallas_call(
        _dn_kernel,
        out_shape=jax.ShapeDtypeStruct((B * S, DN_W), BF16),
        grid=(B, DN_HEADS // hp),
        in_specs=[
            pl.BlockSpec(memory_space=pltpu.SMEM),
            pl.BlockSpec(memory_space=pltpu.SMEM),
            seq(DN_Q_BLK), seq(DN_K_BLK), seq(DN_V_BLK), seq(DN_Z_BLK),
            pl.BlockSpec((S, LANES), lambda b, h: (b, DN_BA_BLK)),
            cw(0), cw(DN_HEADS), cw(2 * DN_HEADS),
            pl.BlockSpec((1, dk), lambda b, h: (0, 0)),
        ],
        out_specs=pl.BlockSpec((S, hp * dk), lambda b, h: (b, h)),
        scratch_shapes=[pltpu.VMEM((S + CONV_PAD, dk), F32)] + [pltpu.VMEM((hp, S, dk), F32)] * 7
        + [pltpu.VMEM((hp, S, DN_CHUNK), F32)] + [pltpu.VMEM((hp, S // DN_CHUNK * dk, dk), F32)] * 2
        + [pltpu.VMEM((hp, S // DN_CHUNK * 8, dk), F32)],
        compiler_params=pltpu.CompilerParams(
            dimension_semantics=("arbitrary", "arbitrary"), vmem_limit_bytes=58 * MIB),
        name="deltanet",
    )(a_log, dt_bias, proj, proj, proj, proj, proj, conv_w, conv_w, conv_w,
      dn_norm_w.reshape(1, dk))


def _pack_bf16_pairs(x):
    c = x.shape[1] // 2
    bits = lambda t: lax.bitcast_convert_type(t.astype(BF16).astype(F32), jnp.uint32)
    return (bits(x[:, :c]) >> 16) | (bits(x[:, c:]) & jnp.uint32(0xFFFF0000))


def _unpack_bf16_pairs(p):
    lo = lax.bitcast_convert_type(p << 16, F32).astype(BF16)
    hi = lax.bitcast_convert_type(p & jnp.uint32(0xFFFF0000), F32).astype(BF16)
    return lo, hi


def _outproj_kernel(a_ref, d_ref, x_ref, wa_ref, wd_ref, nw_ref, wrh_ref, wrl_ref, br_ref,
                    h_ref, u_ref, ti_ref, tg_ref, cnt_ref):
    h = x_ref[...] + _dot(a_ref[...], wa_ref[...]) + _dot(d_ref[...], wd_ref[...])
    h_ref[...] = h
    ms = jnp.mean(h * h, axis=-1, keepdims=True)
    u = h * lax.rsqrt(ms + NORM_EPS) * nw_ref[...]
    u_ref[...] = _pack_bf16_pairs(u)
    u_hi, u_lo = _split(u, 2)
    logits = (_dot(u_hi, wrh_ref[...]) + _dot(u_lo, wrh_ref[...]) + _dot(u_hi, wrl_ref[...])
              + br_ref[...])
    lane = lax.broadcasted_iota(jnp.int32, logits.shape, 1)
    lane_f = lane.astype(F32)
    vals = jnp.where(lane < N_EXPERTS, logits, -jnp.inf)
    top_v, top_i = [], []
    for _ in range(TOP_K):
        m = jnp.max(vals, axis=-1, keepdims=True)
        idx = jnp.min(jnp.where(vals == m, lane_f, float(LANES)), axis=-1, keepdims=True)
        top_v.append(m)
        top_i.append(idx)
        vals = jnp.where(lane_f == idx, -jnp.inf, vals)
    e = [jnp.exp(v - top_v[0]) for v in top_v]
    denom = e[0] + e[1] + e[2] + e[3]
    ti = jnp.zeros(logits.shape, F32)
    tg = jnp.zeros(logits.shape, F32)
    @pl.when(pl.program_id(0) == 0)
    def _():
        cnt_ref[...] = jnp.zeros(cnt_ref.shape, F32)

    tm = logits.shape[0]
    hot = [jnp.where(lane_f == idx, 1.0, 0.0) for idx in top_i]
    hot_all = hot[0] + hot[1] + hot[2] + hot[3]
    earlier = (lax.broadcasted_iota(jnp.int32, (tm, tm), 0)
               > lax.broadcasted_iota(jnp.int32, (tm, tm), 1)).astype(BF16)
    before = cnt_ref[0:1, :] + _dot(earlier, hot_all.astype(BF16))
    cnt_ref[0:1, :] = cnt_ref[0:1, :] + jnp.sum(hot_all, axis=0, keepdims=True)
    for k in range(TOP_K):
        ti = jnp.where(lane == k, top_i[k], ti)
        ti = jnp.where(lane == TOP_K + k, jnp.sum(hot[k] * before, axis=-1, keepdims=True), ti)
        tg = jnp.where(lane == k, e[k] / denom, tg)
    ti_ref[...] = ti.astype(jnp.int32)
    tg_ref[...] = tg


def _outproj_router(attn_o, dn_o, x2d, w_out, ffn_norm_w, w_router, b_router, tm):
    T, D = x2d.shape
    wa = w_out[:ATTN_Q_W].astype(BF16)
    wd = w_out[ATTN_Q_W:].astype(BF16)
    wr = jnp.zeros((D, LANES), F32).at[:, :N_EXPERTS].set(w_router)
    wr_hi = wr.astype(BF16)
    wr_lo = (wr - wr_hi.astype(F32)).astype(BF16)
    br = jnp.zeros((1, LANES), F32).at[0, :N_EXPERTS].set(b_router)
    const = lambda shape: pl.BlockSpec(shape, lambda i: (0, 0))
    tile = lambda w: pl.BlockSpec((tm, w), lambda i: (i, 0))
    return pl.pallas_call(
        _outproj_kernel,
        out_shape=(jax.ShapeDtypeStruct((T, D), F32), jax.ShapeDtypeStruct((T, D // 2), jnp.uint32),
                   jax.ShapeDtypeStruct((T, LANES), jnp.int32),
                   jax.ShapeDtypeStruct((T, LANES), F32), jax.ShapeDtypeStruct((8, LANES), F32)),
        grid=(T // tm,),
        in_specs=[tile(ATTN_Q_W), tile(DN_W), tile(D), const((ATTN_Q_W, D)), const((DN_W, D)),
                  const((1, D)), const((D, LANES)), const((D, LANES)), const((1, LANES))],
        out_specs=(tile(D), tile(D // 2), tile(LANES), tile(LANES), const((8, LANES))),
        compiler_params=pltpu.CompilerParams(
            dimension_semantics=("arbitrary",), vmem_limit_bytes=56 * MIB),
        name="outproj",
    )(attn_o, dn_o, x2d, wa, wd, ffn_norm_w.reshape(1, D), wr_hi, wr_lo, br)


def _experts_kernel(ie_ref, ib_ref, in_ref, used_ref, x_ref, wg_ref, wl_ref, wd_ref,
                    bg_ref, bl_ref, bd_ref, y_ref, xraw, ybuf, wg_s, wl_s, wd_s, sem_x, sem_y):
    w = pl.program_id(0)
    j = pl.program_id(1)
    nw = pl.num_programs(0)
    nj = pl.num_programs(1)
    R = MOE_ROWS
    nb = in_ref[w]
    row0 = ib_ref[w] * R
    slot = lax.rem(w, 2)
    w_next = jnp.minimum(w + 1, nw - 1)
    nb_next = jnp.where(w + 1 < nw, in_ref[w_next], 0)
    row0_next = ib_ref[w_next] * R

    def x_copy(base, s, r):
        return pltpu.make_async_copy(x_ref.at[pl.ds(base + r * R, R), :],
                                     xraw.at[s, pl.ds(r * R, R), :], sem_x.at[s])

    def y_copy(r):
        return pltpu.make_async_copy(ybuf.at[pl.ds(r * R, R), :],
                                     y_ref.at[pl.ds(row0 + r * R, R), :], sem_y)

    def for_range(n, fn):
        def body(r, _):
            fn(r)
            return 0
        lax.fori_loop(0, n, body, 0)

    def rows_of(r):
        return pl.ds(pl.multiple_of(r * R, R), R)

    @pl.when((j == 0) & (w == 0))
    def _():
        for_range(nb, lambda r: x_copy(row0, slot, r).start())

    @pl.when(j == 0)
    def _():
        bd = jnp.broadcast_to(bd_ref[...], (R, ybuf.shape[1]))

        def init(r):
            ybuf[rows_of(r), :] = bd

        for_range(nb, init)
        for_range(nb, lambda r: x_copy(row0, slot, r).wait())
        for_range(nb_next, lambda r: x_copy(row0_next, 1 - slot, r).start())

    @pl.when(nb > 0)
    def _():
        wg_s[...] = wg_ref[...].astype(BF16)
        wl_s[...] = wl_ref[...].astype(BF16)
        wd_s[...] = wd_ref[...].astype(BF16)

    bg = bg_ref[...]
    bl = bl_ref[...]

    def blocks(rs, write_back):
        rows = [rows_of(r) for r in rs]
        xs = [jnp.concatenate(_unpack_bf16_pairs(xraw[slot, rw, :]), axis=1) for rw in rows]
        hg = [jnp.minimum(_dot(x, wg_s[...]) + bg, SWIGLU_LIMIT) for x in xs]
        hl = [jnp.clip(_dot(x, wl_s[...]) + bl, -SWIGLU_LIMIT, SWIGLU_LIMIT) for x in xs]
        acts = [(a * jax.nn.sigmoid(SWIGLU_ALPHA * a) * (b + 1.0)).astype(BF16) for a, b in zip(hg, hl)]
        for rw, act in zip(rows, acts):
            ybuf[rw, :] += _dot(act, wd_s[...])
        if write_back:
            for r in rs:
                y_copy(r).start()

    def all_blocks(write_back):
        lock = MOE_LOCKSTEP_BLOCKS

        def full(i, _):
            blocks([lock * i + t for t in range(lock)], write_back)
            return 0

        lax.fori_loop(0, nb // lock, full, 0)
        base = (nb // lock) * lock
        size = lock // 2
        while size >= 1:
            @pl.when(((nb - base) & size) != 0)
            def _(start=base, size=size):
                blocks([start + t for t in range(size)], write_back)

            base = base + ((nb - base) & size)
            size //= 2

    @pl.when(j < nj - 1)
    def _():
        all_blocks(False)

    @pl.when(j == nj - 1)
    def _():
        all_blocks(True)
        for_range(nb, lambda r: y_copy(r).wait())

    @pl.when((j == nj - 1) & (w == pl.num_programs(0) - 1))
    def _():
        ybuf[0:R, :] = jnp.zeros((R, ybuf.shape[1]), F32)

        def tail_copy(r):
            return pltpu.make_async_copy(ybuf.at[pl.ds(0, R), :], y_ref.at[pl.ds(r * R, R), :], sem_y)

        def tail(fn):
            def body(r, _):
                fn(r)
                return 0
            lax.fori_loop(used_ref[0], y_ref.shape[0] // R, body, 0)

        tail(lambda r: tail_copy(r).start())
        tail(lambda r: tail_copy(r).wait())


def _experts(x_sorted, item_expert, item_blk0, item_nblk, n_blocks_used,
             w_gate_up, b_gate_up, w_down, b_down):
    P = x_sorted.shape[0]
    D = x_sorted.shape[1] * 2
    E, _, F2 = w_gate_up.shape
    F = F2 // 2
    tf = MOE_F_TILE
    nj = F // tf
    n_items = item_expert.shape[0]
    rows = MOE_ITEM_BLOCKS * MOE_ROWS
    b_gu = b_gate_up.reshape(E, 1, F2)
    b_d = b_down.reshape(E, 1, D)
    jt = lambda j, n: jnp.where(n > 0, j, nj - 1)
    return pl.pallas_call(
        _experts_kernel,
        out_shape=jax.ShapeDtypeStruct((P, D), F32),
        grid_spec=pltpu.PrefetchScalarGridSpec(
            num_scalar_prefetch=4,
            grid=(n_items, nj),
            in_specs=[
                pl.BlockSpec(memory_space=pl.ANY),
                pl.BlockSpec((None, D, tf), lambda w, j, ie, ib, nb, *_: (ie[w], 0, jt(j, nb[w]))),
                pl.BlockSpec((None, D, tf), lambda w, j, ie, ib, nb, *_: (ie[w], 0, nj + jt(j, nb[w]))),
                pl.BlockSpec((None, tf, D), lambda w, j, ie, ib, nb, *_: (ie[w], jt(j, nb[w]), 0)),
                pl.BlockSpec((None, 1, tf), lambda w, j, ie, ib, nb, *_: (ie[w], 0, jt(j, nb[w]))),
                pl.BlockSpec((None, 1, tf), lambda w, j, ie, ib, nb, *_: (ie[w], 0, nj + jt(j, nb[w]))),
                pl.BlockSpec((None, 1, D), lambda w, j, ie, *_: (ie[w], 0, 0)),
            ],
            out_specs=pl.BlockSpec(memory_space=pl.ANY),
            scratch_shapes=[
                pltpu.VMEM((2, rows, D // 2), jnp.uint32), pltpu.VMEM((rows, D), F32),
                pltpu.VMEM((D, tf), BF16), pltpu.VMEM((D, tf), BF16), pltpu.VMEM((tf, D), BF16),
                pltpu.SemaphoreType.DMA((2,)), pltpu.SemaphoreType.DMA,
            ],
        ),
        compiler_params=pltpu.CompilerParams(
            dimension_semantics=("arbitrary", "arbitrary"), vmem_limit_bytes=60 * MIB),
        name="experts",
    )(item_expert, item_blk0, item_nblk, n_blocks_used, x_sorted,
      w_gate_up, w_gate_up, w_down, b_gu, b_gu, b_d)


def _combine_kernel(dest_ref, h_ref, tg_ref, y_ref, o_ref, buf, sem):
    i = pl.program_id(0)
    n = pl.num_programs(0)
    tm = h_ref.shape[0]

    def issue_tile(step):
        slot = lax.rem(step, 2)

        def issue(r, _):
            for t in range(ROWS_PER_ISSUE):
                for k in range(TOP_K):
                    row = dest_ref[(step * tm + ROWS_PER_ISSUE * r + t) * TOP_K + k]
                    pltpu.make_async_copy(y_ref.at[pl.ds(row, 1), :],
                                          buf.at[slot, k, pl.ds(ROWS_PER_ISSUE * r + t, 1), :],
                                          sem.at[slot]).start()
            return 0

        lax.fori_loop(0, tm // ROWS_PER_ISSUE, issue, 0)

    @pl.when(i == 0)
    def _():
        issue_tile(0)

    @pl.when(i + 1 < n)
    def _():
        issue_tile(i + 1)

    slot = lax.rem(i, 2)
    for k in range(TOP_K):
        pltpu.make_async_copy(y_ref.at[pl.ds(0, tm), :], buf.at[slot, k], sem.at[slot]).wait()
    tg = tg_ref[...]
    acc = h_ref[...]
    for k in range(TOP_K):
        acc = acc + tg[:, k:k + 1] * buf[slot, k]
    o_ref[...] = acc


def _combine(h, top_g, y_sorted, dest, tm):
    T, D = h.shape
    return pl.pallas_call(
        _combine_kernel,
        out_shape=jax.ShapeDtypeStruct((T, D), F32),
        grid_spec=pltpu.PrefetchScalarGridSpec(
            num_scalar_prefetch=1,
            grid=(T // tm,),
            in_specs=[
                pl.BlockSpec((tm, D), lambda i, d: (i, 0)),
                pl.BlockSpec((tm, LANES), lambda i, d: (i, 0)),
                pl.BlockSpec(memory_space=pl.ANY),
            ],
            out_specs=pl.BlockSpec((tm, D), lambda i, d: (i, 0)),
            scratch_shapes=[pltpu.VMEM((2, TOP_K, tm, D), F32), pltpu.SemaphoreType.DMA((2,))],
        ),
        compiler_params=pltpu.CompilerParams(dimension_semantics=("arbitrary",)),
        name="combine",
    )(dest, h, top_g, y_sorted)


def _dispatch_kernel(dest_ref, row_end_ref, blk_end_ref, u_ref, x_ref, buf, zbuf, sem_l, sem_s, sem_z):
    i = pl.program_id(0)
    n = pl.num_programs(0)
    tm = buf.shape[1]
    R = MOE_ROWS
    n_blocks = x_ref.shape[0] // R

    def load(step):
        s = lax.rem(step, 2)
        return pltpu.make_async_copy(u_ref.at[pl.ds(step * tm, tm), :], buf.at[s], sem_l.at[s])

    def zero_copy(blk):
        return pltpu.make_async_copy(zbuf, x_ref.at[pl.ds(blk * R, R), :], sem_z)

    def wait_scatter(step):
        s = lax.rem(step, 2)
        for _ in range(TOP_K):
            pltpu.make_async_copy(u_ref.at[pl.ds(0, tm), :], buf.at[s], sem_s.at[s]).wait()

    @pl.when(i == 0)
    def _():
        zbuf[...] = jnp.zeros(zbuf.shape, zbuf.dtype)

        def partial_blocks(fn):
            def body(e, _):
                @pl.when(lax.rem(row_end_ref[e], R) != 0)
                def _():
                    fn(blk_end_ref[e] - 1)
                return 0
            lax.fori_loop(0, N_EXPERTS, body, 0)

        def unused_blocks(fn):
            def body(b, _):
                fn(b)
                return 0
            lax.fori_loop(blk_end_ref[N_EXPERTS - 1], n_blocks, body, 0)

        partial_blocks(lambda b: zero_copy(b).start())
        unused_blocks(lambda b: zero_copy(b).start())
        partial_blocks(lambda b: zero_copy(b).wait())
        unused_blocks(lambda b: zero_copy(b).wait())
        load(0).start()

    @pl.when(i > 0)
    def _():
        wait_scatter(i - 1)

    @pl.when(i + 1 < n)
    def _():
        load(i + 1).start()

    load(i).wait()
    slot = lax.rem(i, 2)

    def issue(r, _):
        for t in range(ROWS_PER_ISSUE):
            for k in range(TOP_K):
                row = dest_ref[(i * tm + ROWS_PER_ISSUE * r + t) * TOP_K + k]
                pltpu.make_async_copy(buf.at[slot, pl.ds(ROWS_PER_ISSUE * r + t, 1), :],
                                      x_ref.at[pl.ds(row, 1), :], sem_s.at[slot]).start()
        return 0

    lax.fori_loop(0, tm // ROWS_PER_ISSUE, issue, 0)

    @pl.when(i == n - 1)
    def _():
        wait_scatter(i)


def _dispatch(u_packed, dest, row_end, blk_end, P, tm):
    T, W = u_packed.shape
    return pl.pallas_call(
        _dispatch_kernel,
        out_shape=jax.ShapeDtypeStruct((P, W), u_packed.dtype),
        grid_spec=pltpu.PrefetchScalarGridSpec(
            num_scalar_prefetch=3,
            grid=(T // tm,),
            in_specs=[pl.BlockSpec(memory_space=pl.ANY)],
            out_specs=pl.BlockSpec(memory_space=pl.ANY),
            scratch_shapes=[pltpu.VMEM((2, tm, W), u_packed.dtype), pltpu.VMEM((MOE_ROWS, W), u_packed.dtype),
                            pltpu.SemaphoreType.DMA((2,)), pltpu.SemaphoreType.DMA((2,)),
                            pltpu.SemaphoreType.DMA],
        ),
        compiler_params=pltpu.CompilerParams(dimension_semantics=("arbitrary",)),
        name="dispatch",
    )(dest, row_end, blk_end, u_packed)


def _routing(top_i, expert_counts, T):
    R = MOE_ROWS
    n_assign = T * TOP_K
    e_flat = top_i[:, :TOP_K].reshape(n_assign)
    rank = top_i[:, TOP_K:2 * TOP_K].reshape(n_assign)
    counts = expert_counts[0, :N_EXPERTS].astype(jnp.int32)
    nblk = (counts + R - 1) // R
    blk_end = jnp.cumsum(nblk)
    blk_start = blk_end - nblk
    dest = blk_start[e_flat] * R + rank
    max_blocks = n_assign // R + N_EXPERTS
    P = max_blocks * R
    dest = dest.astype(jnp.int32)
    row_end = (blk_start * R + counts).astype(jnp.int32)
    nb_item = MOE_ITEM_BLOCKS
    n_items_e = (nblk + nb_item - 1) // nb_item
    item_end = jnp.cumsum(n_items_e)
    item_start = item_end - n_items_e
    max_items = (max_blocks + (nb_item - 1) * N_EXPERTS) // nb_item
    w = jnp.arange(max_items, dtype=jnp.int32)
    ie = jnp.minimum(jnp.searchsorted(item_end, w, side='right'), N_EXPERTS - 1).astype(jnp.int32)
    local = w - item_start[ie]
    live = w < item_end[-1]
    ib = blk_start[ie] + local * nb_item
    inb = jnp.clip(nblk[ie] - local * nb_item, 0, nb_item)
    last_e = ie[jnp.maximum(item_end[-1] - 1, 0)]
    ie = jnp.where(live, ie, last_e).astype(jnp.int32)
    ib = jnp.where(live, ib, 0).astype(jnp.int32)
    inb = jnp.where(live, inb, 0).astype(jnp.int32)
    return dest, row_end, blk_end.astype(jnp.int32), P, ie, ib, inb


def _layer(x, attn_norm_w, w_in, q_norm_w, k_norm_w, attn_sinks, conv_w, a_log, dt_bias,
           dn_norm_w, w_out, ffn_norm_w, w_router, b_router, w_gate_up, b_gate_up, w_down, b_down):
    B, S, D = x.shape
    T = B * S
    x2d = x.reshape(T, D)
    proj = _inproj(x2d, attn_norm_w, w_in, tm=min(T, 1024), tn=1024)
    attn_o = _attention(proj, attn_sinks, q_norm_w, k_norm_w, B, S)
    dn_o = _deltanet(proj, conv_w, a_log, dt_bias, dn_norm_w, B, S)
    h, u, top_i, top_g, expert_counts = _outproj_router(attn_o, dn_o, x2d, w_out, ffn_norm_w, w_router,
                                                        b_router, tm=min(T, 512))
    dest, row_end, blk_end, P, ie, ib, inb = _routing(top_i, expert_counts, T)
    x_sorted = _dispatch(u, dest, row_end, blk_end, P, tm=min(T, 256))
    y_sorted = _experts(x_sorted, ie, ib, inb, blk_end[-1:], w_gate_up, b_gate_up, w_down, b_down)
    out = _combine(h, top_g, y_sorted, dest, tm=min(T, 128))
    return out.reshape(B, S, D)


def kernel(x, attn_norm_w, w_in, q_norm_w, k_norm_w, attn_sinks, conv_w, a_log, dt_bias, dn_norm_w,
           w_out, ffn_norm_w, w_router, b_router, w_gate_up, b_gate_up, w_down, b_down):
    h = x
    for l in range(attn_norm_w.shape[0]):
        h = _layer(h, attn_norm_w[l], w_in[l], q_norm_w[l], k_norm_w[l], attn_sinks[l], conv_w[l],
                   a_log[l], dt_bias[l], dn_norm_w[l], w_out[l], ffn_norm_w[l], w_router[l],
                   b_router[l], w_gate_up[l], b_gate_up[l], w_down[l], b_down[l])
    return h
```

```python
import functools
import math

import jax
import jax.numpy as jnp
from jax import lax
from jax.experimental import pallas as pl
from jax.experimental.pallas import tpu as pltpu

F32 = jnp.float32
BF16 = jnp.bfloat16

D_MODEL = 2048
ATTN_HEAD_DIM = 64
ATTN_HEADS = 16
ATTN_KV_HEADS = 4
ATTN_GROUP = ATTN_HEADS // ATTN_KV_HEADS
WINDOW = 128
DN_HEAD_DIM = 128
DN_HEADS = 8
DN_CHUNK = 64
CONV_WIDTH = 4
ATTN_Q_W = ATTN_HEADS * ATTN_HEAD_DIM
ATTN_KV_W = ATTN_KV_HEADS * ATTN_HEAD_DIM
DN_W = DN_HEADS * DN_HEAD_DIM
IN_WIDTH = ATTN_Q_W + 2 * ATTN_KV_W + 4 * DN_W + 2 * DN_HEADS
N_EXPERTS = 32
TOP_K = 4
D_EXPERT = 2048
SWIGLU_ALPHA = 1.702
SWIGLU_LIMIT = 7.0
NORM_EPS = 1e-5
QK_NORM_EPS = 1e-6
DN_NORM_EPS = 1e-6
L2_EPS = 1e-6

LANES = 128
MIB = 1024 * 1024

ATTN_K_BLK = ATTN_Q_W // ATTN_KV_W
ATTN_V_BLK = ATTN_K_BLK + 1
DN_Q_BLK = (ATTN_Q_W + 2 * ATTN_KV_W) // DN_HEAD_DIM
DN_K_BLK = DN_Q_BLK + DN_HEADS
DN_V_BLK = DN_K_BLK + DN_HEADS
DN_Z_BLK = DN_V_BLK + DN_HEADS
DN_BA_BLK = DN_Z_BLK + DN_HEADS

MOE_ROWS = 256
MOE_ITEM_BLOCKS = 9
MOE_F_TILE = 256
MOE_LOCKSTEP_BLOCKS = 4
ROWS_PER_ISSUE = 4


def _dot(a, b):
    return jnp.dot(a, b, preferred_element_type=F32)


def _dot_nt(a, b):
    return lax.dot_general(a, b, (((1,), (1,)), ((), ())), preferred_element_type=F32)


def _dot_tn(a, b):
    return lax.dot_general(a, b, (((0,), (0,)), ((), ())), preferred_element_type=F32)


def _inproj_kernel(x_ref, nw_ref, w_ref, o_ref, u_ref, *, last_width):
    @pl.when(pl.program_id(1) == 0)
    def _():
        x = x_ref[...]
        ms = jnp.mean(x * x, axis=-1, keepdims=True)
        u_ref[...] = (x * lax.rsqrt(ms + NORM_EPS) * nw_ref[...]).astype(BF16)

    last = pl.num_programs(1) - 1

    @pl.when(pl.program_id(1) < last)
    def _():
        o_ref[...] = _dot(u_ref[...], w_ref[...])

    @pl.when(pl.program_id(1) == last)
    def _():
        o_ref[:, :last_width] = _dot(u_ref[...], w_ref[:, :last_width])


def _inproj(x2d, norm_w, w_in, tm, tn):
    T, D = x2d.shape
    N = w_in.shape[1]
    n_col_blocks = pl.cdiv(N, tn)
    last_width = min(tn, -(-(N - (n_col_blocks - 1) * tn) // LANES) * LANES)
    return pl.pallas_call(
        functools.partial(_inproj_kernel, last_width=last_width),
        out_shape=jax.ShapeDtypeStruct((T, N), F32),
        grid=(T // tm, pl.cdiv(N, tn)),
        in_specs=[
            pl.BlockSpec((tm, D), lambda i, j: (i, 0)),
            pl.BlockSpec((1, D), lambda i, j: (0, 0)),
            pl.BlockSpec((D, tn), lambda i, j: (0, j)),
        ],
        out_specs=pl.BlockSpec((tm, tn), lambda i, j: (i, j)),
        scratch_shapes=[pltpu.VMEM((tm, D), BF16)],
        compiler_params=pltpu.CompilerParams(
            dimension_semantics=("arbitrary", "arbitrary"), vmem_limit_bytes=56 * MIB),
        name="inproj",
    )(x2d, norm_w.reshape(1, D), w_in.astype(BF16))


def _attn_kernel(sink_ref, q_ref, kp_ref, kc_ref, vp_ref, vc_ref, qw_ref, kw_ref, o_ref):
    n = pl.program_id(1)
    L = WINDOW
    d = ATTN_HEAD_DIM

    def head_norm(t, w):
        ms = jnp.mean(t * t, axis=-1, keepdims=True)
        return t * lax.rsqrt(ms + QK_NORM_EPS) * w

    G = ATTN_GROUP
    qi = lax.broadcasted_iota(jnp.int32, (L, 2 * L), 0)
    kj = lax.broadcasted_iota(jnp.int32, (L, 2 * L), 1)
    dist = qi + L - kj
    valid = (dist >= 0) & (dist < WINDOW) & ((kj >= L) | (n > 0))
    dist_f = dist.astype(F32)

    qw = qw_ref[...]
    kw = kw_ref[...]
    kvs = range(ATTN_KV_HEADS)
    heads = lambda kv: [kv * G + g for g in range(G)]
    cols = lambda h: slice(h * d, (h + 1) * d)
    k = [head_norm(jnp.concatenate([kp_ref[:, cols(kv)], kc_ref[:, cols(kv)]], axis=0), kw).astype(BF16)
         for kv in kvs]
    v = [jnp.concatenate([vp_ref[:, cols(kv)], vc_ref[:, cols(kv)]], axis=0).astype(BF16) for kv in kvs]
    q = [head_norm(jnp.concatenate([q_ref[:, cols(h)] for h in heads(kv)], axis=0), qw).astype(BF16)
         for kv in kvs]
    s4 = [_dot_nt(q[kv], k[kv]) for kv in kvs]
    hs = range(ATTN_HEADS)
    s = [s4[h // G][(h % G) * L:(h % G + 1) * L, :] for h in hs]
    s = [jnp.where(valid, s[h] * (d ** -0.5) - 2.0 ** (-8.0 * (h + 1) / ATTN_HEADS) * dist_f, -jnp.inf)
         for h in hs]
    m = [jnp.maximum(jnp.max(s[h], axis=-1, keepdims=True), sink_ref[h]) for h in hs]
    p = [jnp.exp(s[h] - m[h]) for h in hs]
    denom = [jnp.sum(p[h], axis=-1, keepdims=True) + jnp.exp(sink_ref[h] - m[h]) for h in hs]
    probs = [(p[h] / denom[h]).astype(BF16) for h in hs]
    o = [_dot(jnp.concatenate([probs[h] for h in heads(kv)], axis=0), v[kv]) for kv in kvs]
    for h in hs:
        o_ref[:, cols(h)] = o[h // G][(h % G) * L:(h % G + 1) * L, :].astype(o_ref.dtype)


def _attention(proj, sinks, q_norm_w, k_norm_w, B, S):
    nb = S // WINDOW
    L = WINDOW
    row = lambda b, n: b * nb + n
    prev = lambda b, n: b * nb + jnp.maximum(n - 1, 0)
    return pl.pallas_call(
        _attn_kernel,
        out_shape=jax.ShapeDtypeStruct((B * S, ATTN_Q_W), BF16),
        grid=(B, nb),
        in_specs=[
            pl.BlockSpec(memory_space=pltpu.SMEM),
            pl.BlockSpec((L, ATTN_Q_W), lambda b, n: (row(b, n), 0)),
            pl.BlockSpec((L, ATTN_KV_W), lambda b, n: (prev(b, n), ATTN_K_BLK)),
            pl.BlockSpec((L, ATTN_KV_W), lambda b, n: (row(b, n), ATTN_K_BLK)),
            pl.BlockSpec((L, ATTN_KV_W), lambda b, n: (prev(b, n), ATTN_V_BLK)),
            pl.BlockSpec((L, ATTN_KV_W), lambda b, n: (row(b, n), ATTN_V_BLK)),
            pl.BlockSpec((1, ATTN_HEAD_DIM), lambda b, n: (0, 0)),
            pl.BlockSpec((1, ATTN_HEAD_DIM), lambda b, n: (0, 0)),
        ],
        out_specs=pl.BlockSpec((L, ATTN_Q_W), lambda b, n: (row(b, n), 0)),
        compiler_params=pltpu.CompilerParams(dimension_semantics=("arbitrary", "arbitrary")),
        name="attn",
    )(sinks, proj, proj, proj, proj, proj, q_norm_w.reshape(1, -1), k_norm_w.reshape(1, -1))


CONV_PAD = 8


DN_GROUP = 16

_X3 = ((0, 0), (1, 0), (0, 1))
_EXACT_LHS = ((0, 0), (0, 1), (0, 2))
_EXACT_RHS = ((0, 0), (1, 0), (2, 0))


def _split(x, n):
    parts = []
    for i in range(n):
        p = x.astype(BF16)
        parts.append(p)
        if i + 1 < n:
            x = x - p.astype(F32)
    return parts


def _mm(a_parts, b_parts, terms, form='nn'):
    a_axis = 0 if form == 'tn' else 1
    b_axis = 1 if form == 'nt' else 0
    a = jnp.concatenate([a_parts[i] for i, _ in terms], axis=a_axis)
    b = jnp.concatenate([b_parts[j] for _, j in terms], axis=b_axis)
    return {'nn': _dot, 'nt': _dot_nt, 'tn': _dot_tn}[form](a, b)


def _dn_front(h, alog_ref, dtb_ref, q_ref, k_ref, v_ref, ba_ref, cq_ref, ck_ref, cv_ref,
              xp, qs, ks, vs, gs, bs, us, ws, am, pm, qm, gl):
    S = q_ref.shape[0]
    C = DN_CHUNK
    dk = DN_HEAD_DIM
    piece = min(S, 256)

    def conv_silu(x_ref, cw_ref, dst, l2):
        xp[0:CONV_PAD, :] = jnp.zeros((CONV_PAD, dk), F32)
        xp[CONV_PAD:CONV_PAD + S, :] = x_ref[...]
        for p in range(S // piece):
            r0 = p * piece
            acc = jnp.zeros((piece, dk), F32)
            for i in range(CONV_WIDTH):
                off = CONV_PAD + r0 - (CONV_WIDTH - 1) + i
                acc = acc + xp[off:off + piece, :] * cw_ref[i:i + 1, :]
            y = acc * jax.nn.sigmoid(acc)
            if l2:
                y = y * lax.rsqrt(jnp.sum(y * y, axis=-1, keepdims=True) + L2_EPS)
            dst[r0:r0 + piece, :] = y

    conv_silu(q_ref, cq_ref, qs, True)
    conv_silu(k_ref, ck_ref, ks, True)
    conv_silu(v_ref, cv_ref, vs, False)

    neg_a = -jnp.exp(jnp.full((1, dk), alog_ref[h], F32))
    dtb = dtb_ref[h]
    for p in range(S // piece):
        r0 = p * piece
        ba = ba_ref[r0:r0 + piece, :]
        lane = lax.broadcasted_iota(jnp.int32, ba.shape, 1)
        b_col = jnp.sum(jnp.where(lane == h, ba, 0.0), axis=-1, keepdims=True)
        a_col = jnp.sum(jnp.where(lane == DN_HEADS + h, ba, 0.0), axis=-1, keepdims=True)
        bs[r0:r0 + piece, :] = jnp.broadcast_to(jax.nn.sigmoid(b_col), (piece, dk))
        gs[r0:r0 + piece, :] = neg_a * jnp.broadcast_to(jax.nn.softplus(a_col + dtb), (piece, dk))

    ri = lax.broadcasted_iota(jnp.int32, (C, C), 0)
    ci = lax.broadcasted_iota(jnp.int32, (C, C), 1)
    incl = ri >= ci
    strict = ri > ci
    tri_ones = [jnp.where(incl, 1.0, 0.0).astype(BF16)]
    upper_ones = [jnp.where(ri <= ci, 1.0, 0.0).astype(BF16)]
    n_chunks = S // C
    group = DN_GROUP if n_chunks % DN_GROUP == 0 else 1

    def intra_load(c):
        rows = pl.ds(pl.multiple_of(c * C, C), C)
        return qs[rows, :], ks[rows, :], vs[rows, :], gs[rows, :], bs[rows, :]

    def intra_compute(loaded):
        each = lambda f, *ls: [f(*a) for a in zip(*ls)]
        q, k, v, g, beta = (list(t) for t in zip(*loaded))
        q = each(lambda t: t * (dk ** -0.5), q)
        g3 = each(lambda t: _split(t, 3), g)
        gc = each(lambda p: _mm(tri_ones, p, _EXACT_LHS), g3)
        gc_row = each(lambda p: _mm([t[:, :C] for t in p], upper_ones, _EXACT_RHS, 'tn'), g3)
        decay = each(lambda a, b: jnp.exp(jnp.where(incl, a[:, :C] - b, -jnp.inf)), gc, gc_row)
        kb = each(jnp.multiply, k, beta)
        k2 = each(lambda t: _split(t, 2), k)
        kk = each(lambda a, b: _mm(_split(a, 2), b, _X3, 'nt'), kb, k2)
        qk = each(lambda a, b: _mm(_split(a, 2), b, _X3, 'nt'), q, k2)
        m = each(lambda a, d: -jnp.where(strict, a * d, 0.0), kk, decay)
        x = each(lambda vv, bb, kbb, gg: jnp.concatenate([vv * bb, kbb * jnp.exp(gg)], axis=1),
                 v, beta, kb, gc)
        n_fac = int(math.log2(C))
        for it in range(n_fac):
            m2 = each(lambda t: _split(t, 2), m)
            x = each(lambda xx, mm: xx + _mm(mm, _split(xx, 2), _X3), x, m2)
            if it + 1 < n_fac:
                m = each(lambda mm: _mm(mm, mm, _X3), m2)
        kd2 = each(lambda kk_, gg: _split(kk_ * jnp.exp(gg[C - 1:C, :] - gg), 2), k, gc)
        pq = each(lambda a, xx: _mm(a, _split(xx, 2), _X3, 'tn'), kd2, x)
        a_intra = each(lambda a, d: jnp.where(incl, a * d, 0.0), qk, decay)
        qg = each(lambda a, gg: a * jnp.exp(gg), q, gc)
        e_last = each(lambda gg: jnp.broadcast_to(jnp.exp(gg[C - 1:C, :]), (8, dk)), gc)
        return list(zip(x, qg, a_intra, pq, e_last))

    def intra_store(c, x, qg, a_intra, pq, e_last):
        rows = pl.ds(pl.multiple_of(c * C, C), C)
        prow = pl.ds(pl.multiple_of(c * dk, dk), dk)
        us[rows, :] = x[:, :dk]
        ws[rows, :] = x[:, dk:]
        qs[rows, :] = qg
        am[rows, :] = a_intra
        qm[prow, :] = pq[:, :dk]
        pm[prow, :] = pq[:, dk:]
        gl[pl.ds(pl.multiple_of(c * 8, 8), 8), :] = e_last

    def intra_group(i, _):
        loaded = [intra_load(i * group + t) for t in range(group)]
        outs = intra_compute(loaded)
        for t in range(group):
            intra_store(i * group + t, *outs[t])
        return 0

    lax.fori_loop(0, n_chunks // group, intra_group, 0)


def _dn_scan(heads):
    dk = DN_HEAD_DIM
    n_chunks = heads[0][0].shape[0] // dk

    def scan(c, states):
        prow = pl.ds(pl.multiple_of(c * dk, dk), dk)
        ps = [_mm(_split(pm[prow, :], 2), _split(state, 2), _X3) for (pm, _, _), state in zip(heads, states)]
        for (pm, _, _), state in zip(heads, states):
            pm[prow, :] = state
        e_last = [gl[pl.ds(pl.multiple_of(c * 8, 8), 1), :] for _, _, gl in heads]
        return tuple(state * e - p + qm[prow, :]
                     for (_, qm, _), state, e, p in zip(heads, states, e_last, ps))

    lax.fori_loop(0, n_chunks, scan, tuple(jnp.zeros((dk, dk), F32) for _ in heads), unroll=2)


def _dn_back(z_ref, nw_ref, o_ref, qs, us, ws, am, pm):
    S = z_ref.shape[0]
    C = DN_CHUNK
    dk = DN_HEAD_DIM
    n_chunks = S // C
    group = DN_GROUP if n_chunks % DN_GROUP == 0 else 1
    nw = nw_ref[...]

    def out_group(i, _):
        cs = [i * group + t for t in range(group)]
        rows = [pl.ds(pl.multiple_of(c * C, C), C) for c in cs]
        each = lambda f, *ls: [f(*a) for a in zip(*ls)]
        state2 = each(lambda c: _split(pm[pl.ds(pl.multiple_of(c * dk, dk), dk), :], 2), cs)
        lhs2 = each(lambda r: _split(jnp.concatenate([ws[r, :], qs[r, :]], axis=0), 2), rows)
        prod = each(lambda a, b: _mm(a, b, _X3), lhs2, state2)
        v_new = each(lambda r, p: us[r, :] - p[:C], rows, prod)
        av = each(lambda r, vn: _mm(_split(am[r, :], 2), _split(vn, 2), _X3), rows, v_new)
        o = each(lambda p, a: p[C:] + a, prod, av)
        o = each(lambda t: t * lax.rsqrt(jnp.mean(t * t, axis=-1, keepdims=True) + DN_NORM_EPS) * nw, o)
        for r, t in zip(rows, o):
            z = z_ref[r, :]
            o_ref[r, :] = (t * (z * jax.nn.sigmoid(z))).astype(o_ref.dtype)
        return 0

    lax.fori_loop(0, n_chunks // group, out_group, 0)


DN_HEADS_PER_STEP = 2


def _dn_kernel(alog_ref, dtb_ref, q_ref, k_ref, v_ref, z_ref, ba_ref, cq_ref, ck_ref, cv_ref,
               nw_ref, o_ref, xp, *scratch):
    dk = DN_HEAD_DIM
    n_heads = o_ref.shape[1] // dk
    lanes = lambda ref, hh: ref.at[pl.ds(0, ref.shape[0]), pl.ds(hh * dk, dk)]
    views = [tuple(r.at[hh] for r in scratch) for hh in range(n_heads)]
    for hh in range(n_heads):
        _dn_front(pl.program_id(1) * n_heads + hh, alog_ref, dtb_ref, lanes(q_ref, hh), lanes(k_ref, hh),
                  lanes(v_ref, hh), ba_ref, lanes(cq_ref, hh), lanes(ck_ref, hh), lanes(cv_ref, hh),
                  xp, *views[hh])
    _dn_scan([(pm, qm, gl) for (qs, ks, vs, gs, bs, us, ws, am, pm, qm, gl) in views])
    for hh, (qs, ks, vs, gs, bs, us, ws, am, pm, qm, gl) in enumerate(views):
        _dn_back(lanes(z_ref, hh), nw_ref, lanes(o_ref, hh), qs, us, ws, am, pm)


def _deltanet(proj, conv_w, a_log, dt_bias, dn_norm_w, B, S):
    dk = DN_HEAD_DIM
    hp = DN_HEADS_PER_STEP
    seq = lambda blk: pl.BlockSpec((S, hp * dk), lambda b, h: (b, blk // hp + h))
    cw = lambda blk: pl.BlockSpec((CONV_WIDTH, hp * dk), lambda b, h: (0, blk // hp + h))
    assert all(blk % hp == 0 for blk in (DN_Q_BLK, DN_K_BLK, DN_V_BLK, DN_Z_BLK, DN_HEADS))
    return pl.pallas_call(
        _dn_kernel,
        out_shape=jax.ShapeDtypeStruct((B * S, DN_W), BF16),
        grid=(B, DN_HEADS // hp),
        in_specs=[
            pl.BlockSpec(memory_space=pltpu.SMEM),
            pl.BlockSpec(memory_space=pltpu.SMEM),
            seq(DN_Q_BLK), seq(DN_K_BLK), seq(DN_V_BLK), seq(DN_Z_BLK),
            pl.BlockSpec((S, LANES), lambda b, h: (b, DN_BA_BLK)),
            cw(0), cw(DN_HEADS), cw(2 * DN_HEADS),
            pl.BlockSpec((1, dk), lambda b, h: (0, 0)),
        ],
        out_specs=pl.BlockSpec((S, hp * dk), lambda b, h: (b, h)),
        scratch_shapes=[pltpu.VMEM((S + CONV_PAD, dk), F32)] + [pltpu.VMEM((hp, S, dk), F32)] * 7
        + [pltpu.VMEM((hp, S, DN_CHUNK), F32)] + [pltpu.VMEM((hp, S // DN_CHUNK * dk, dk), F32)] * 2
        + [pltpu.VMEM((hp, S // DN_CHUNK * 8, dk), F32)],
        compiler_params=pltpu.CompilerParams(
            dimension_semantics=("arbitrary", "arbitrary"), vmem_limit_bytes=58 * MIB),
        name="deltanet",
    )(a_log, dt_bias, proj, proj, proj, proj, proj, conv_w, conv_w, conv_w,
      dn_norm_w.reshape(1, dk))


def _pack_bf16_pairs(x):
    c = x.shape[1] // 2
    bits = lambda t: lax.bitcast_convert_type(t.astype(BF16).astype(F32), jnp.uint32)
    return (bits(x[:, :c]) >> 16) | (bits(x[:, c:]) & jnp.uint32(0xFFFF0000))


def _unpack_bf16_pairs(p):
    lo = lax.bitcast_convert_type(p << 16, F32).astype(BF16)
    hi = lax.bitcast_convert_type(p & jnp.uint32(0xFFFF0000), F32).astype(BF16)
    return lo, hi


def _outproj_kernel(a_ref, d_ref, x_ref, wa_ref, wd_ref, nw_ref, wrh_ref, wrl_ref, br_ref,
                    h_ref, u_ref, ti_ref, tg_ref, cnt_ref):
    h = x_ref[...] + _dot(a_ref[...], wa_ref[...]) + _dot(d_ref[...], wd_ref[...])
    h_ref[...] = h
    ms = jnp.mean(h * h, axis=-1, keepdims=True)
    u = h * lax.rsqrt(ms + NORM_EPS) * nw_ref[...]
    u_ref[...] = _pack_bf16_pairs(u)
    u_hi, u_lo = _split(u, 2)
    logits = (_dot(u_hi, wrh_ref[...]) + _dot(u_lo, wrh_ref[...]) + _dot(u_hi, wrl_ref[...])
              + br_ref[...])
    lane = lax.broadcasted_iota(jnp.int32, logits.shape, 1)
    lane_f = lane.astype(F32)
    vals = jnp.where(lane < N_EXPERTS, logits, -jnp.inf)
    top_v, top_i = [], []
    for _ in range(TOP_K):
        m = jnp.max(vals, axis=-1, keepdims=True)
        idx = jnp.min(jnp.where(vals == m, lane_f, float(LANES)), axis=-1, keepdims=True)
        top_v.append(m)
        top_i.append(idx)
        vals = jnp.where(lane_f == idx, -jnp.inf, vals)
    e = [jnp.exp(v - top_v[0]) for v in top_v]
    denom = e[0] + e[1] + e[2] + e[3]
    ti = jnp.zeros(logits.shape, F32)
    tg = jnp.zeros(logits.shape, F32)
    @pl.when(pl.program_id(0) == 0)
    def _():
        cnt_ref[...] = jnp.zeros(cnt_ref.shape, F32)

    tm = logits.shape[0]
    hot = [jnp.where(lane_f == idx, 1.0, 0.0) for idx in top_i]
    hot_all = hot[0] + hot[1] + hot[2] + hot[3]
    earlier = (lax.broadcasted_iota(jnp.int32, (tm, tm), 0)
               > lax.broadcasted_iota(jnp.int32, (tm, tm), 1)).astype(BF16)
    before = cnt_ref[0:1, :] + _dot(earlier, hot_all.astype(BF16))
    cnt_ref[0:1, :] = cnt_ref[0:1, :] + jnp.sum(hot_all, axis=0, keepdims=True)
    for k in range(TOP_K):
        ti = jnp.where(lane == k, top_i[k], ti)
        ti = jnp.where(lane == TOP_K + k, jnp.sum(hot[k] * before, axis=-1, keepdims=True), ti)
        tg = jnp.where(lane == k, e[k] / denom, tg)
    ti_ref[...] = ti.astype(jnp.int32)
    tg_ref[...] = tg


def _outproj_router(attn_o, dn_o, x2d, w_out, ffn_norm_w, w_router, b_router, tm):
    T, D = x2d.shape
    wa = w_out[:ATTN_Q_W].astype(BF16)
    wd = w_out[ATTN_Q_W:].astype(BF16)
    wr = jnp.zeros((D, LANES), F32).at[:, :N_EXPERTS].set(w_router)
    wr_hi = wr.astype(BF16)
    wr_lo = (wr - wr_hi.astype(F32)).astype(BF16)
    br = jnp.zeros((1, LANES), F32).at[0, :N_EXPERTS].set(b_router)
    const = lambda shape: pl.BlockSpec(shape, lambda i: (0, 0))
    tile = lambda w: pl.BlockSpec((tm, w), lambda i: (i, 0))
    return pl.pallas_call(
        _outproj_kernel,
        out_shape=(jax.ShapeDtypeStruct((T, D), F32), jax.ShapeDtypeStruct((T, D // 2), jnp.uint32),
                   jax.ShapeDtypeStruct((T, LANES), jnp.int32),
                   jax.ShapeDtypeStruct((T, LANES), F32), jax.ShapeDtypeStruct((8, LANES), F32)),
        grid=(T // tm,),
        in_specs=[tile(ATTN_Q_W), tile(DN_W), tile(D), const((ATTN_Q_W, D)), const((DN_W, D)),
                  const((1, D)), const((D, LANES)), const((D, LANES)), const((1, LANES))],
        out_specs=(tile(D), tile(D // 2), tile(LANES), tile(LANES), const((8, LANES))),
        compiler_params=pltpu.CompilerParams(
            dimension_semantics=("arbitrary",), vmem_limit_bytes=56 * MIB),
        name="outproj",
    )(attn_o, dn_o, x2d, wa, wd, ffn_norm_w.reshape(1, D), wr_hi, wr_lo, br)


def _experts_kernel(ie_ref, ib_ref, in_ref, used_ref, x_ref, wg_ref, wl_ref, wd_ref,
                    bg_ref, bl_ref, bd_ref, y_ref, xraw, ybuf, wg_s, wl_s, wd_s, sem_x, sem_y):
    w = pl.program_id(0)
    j = pl.program_id(1)
    nw = pl.num_programs(0)
    nj = pl.num_programs(1)
    R = MOE_ROWS
    nb = in_ref[w]
    row0 = ib_ref[w] * R
    slot = lax.rem(w, 2)
    w_next = jnp.minimum(w + 1, nw - 1)
    nb_next = jnp.where(w + 1 < nw, in_ref[w_next], 0)
    row0_next = ib_ref[w_next] * R

    def x_copy(base, s, r):
        return pltpu.make_async_copy(x_ref.at[pl.ds(base + r * R, R), :],
                                     xraw.at[s, pl.ds(r * R, R), :], sem_x.at[s])

    def y_copy(r):
        return pltpu.make_async_copy(ybuf.at[pl.ds(r * R, R), :],
                                     y_ref.at[pl.ds(row0 + r * R, R), :], sem_y)

    def for_range(n, fn):
        def body(r, _):
            fn(r)
            return 0
        lax.fori_loop(0, n, body, 0)

    def rows_of(r):
        return pl.ds(pl.multiple_of(r * R, R), R)

    @pl.when((j == 0) & (w == 0))
    def _():
        for_range(nb, lambda r: x_copy(row0, slot, r).start())

    @pl.when(j == 0)
    def _():
        bd = jnp.broadcast_to(bd_ref[...], (R, ybuf.shape[1]))

        def init(r):
            ybuf[rows_of(r), :] = bd

        for_range(nb, init)
        for_range(nb, lambda r: x_copy(row0, slot, r).wait())
        for_range(nb_next, lambda r: x_copy(row0_next, 1 - slot, r).start())

    @pl.when(nb > 0)
    def _():
        wg_s[...] = wg_ref[...].astype(BF16)
        wl_s[...] = wl_ref[...].astype(BF16)
        wd_s[...] = wd_ref[...].astype(BF16)

    bg = bg_ref[...]
    bl = bl_ref[...]

    def blocks(rs, write_back):
        rows = [rows_of(r) for r in rs]
        xs = [jnp.concatenate(_unpack_bf16_pairs(xraw[slot, rw, :]), axis=1) for rw in rows]
        hg = [jnp.minimum(_dot(x, wg_s[...]) + bg, SWIGLU_LIMIT) for x in xs]
        hl = [jnp.clip(_dot(x, wl_s[...]) + bl, -SWIGLU_LIMIT, SWIGLU_LIMIT) for x in xs]
        acts = [(a * jax.nn.sigmoid(SWIGLU_ALPHA * a) * (b + 1.0)).astype(BF16) for a, b in zip(hg, hl)]
        for rw, act in zip(rows, acts):
            ybuf[rw, :] += _dot(act, wd_s[...])
        if write_back:
            for r in rs:
                y_copy(r).start()

    def all_blocks(write_back):
        lock = MOE_LOCKSTEP_BLOCKS

        def full(i, _):
            blocks([lock * i + t for t in range(lock)], write_back)
            return 0

        lax.fori_loop(0, nb // lock, full, 0)
        base = (nb // lock) * lock
        size = lock // 2
        while size >= 1:
            @pl.when(((nb - base) & size) != 0)
            def _(start=base, size=size):
                blocks([start + t for t in range(size)], write_back)

            base = base + ((nb - base) & size)
            size //= 2

    @pl.when(j < nj - 1)
    def _():
        all_blocks(False)

    @pl.when(j == nj - 1)
    def _():
        all_blocks(True)
        for_range(nb, lambda r: y_copy(r).wait())

    @pl.when((j == nj - 1) & (w == pl.num_programs(0) - 1))
    def _():
        ybuf[0:R, :] = jnp.zeros((R, ybuf.shape[1]), F32)

        def tail_copy(r):
            return pltpu.make_async_copy(ybuf.at[pl.ds(0, R), :], y_ref.at[pl.ds(r * R, R), :], sem_y)

        def tail(fn):
            def body(r, _):
                fn(r)
                return 0
            lax.fori_loop(used_ref[0], y_ref.shape[0] // R, body, 0)

        tail(lambda r: tail_copy(r).start())
        tail(lambda r: tail_copy(r).wait())


def _experts(x_sorted, item_expert, item_blk0, item_nblk, n_blocks_used,
             w_gate_up, b_gate_up, w_down, b_down):
    P = x_sorted.shape[0]
    D = x_sorted.shape[1] * 2
    E, _, F2 = w_gate_up.shape
    F = F2 // 2
    tf = MOE_F_TILE
    nj = F // tf
    n_items = item_expert.shape[0]
    rows = MOE_ITEM_BLOCKS * MOE_ROWS
    b_gu = b_gate_up.reshape(E, 1, F2)
    b_d = b_down.reshape(E, 1, D)
    jt = lambda j, n: jnp.where(n > 0, j, nj - 1)
    return pl.pallas_call(
        _experts_kernel,
        out_shape=jax.ShapeDtypeStruct((P, D), F32),
        grid_spec=pltpu.PrefetchScalarGridSpec(
            num_scalar_prefetch=4,
            grid=(n_items, nj),
            in_specs=[
                pl.BlockSpec(memory_space=pl.ANY),
                pl.BlockSpec((None, D, tf), lambda w, j, ie, ib, nb, *_: (ie[w], 0, jt(j, nb[w]))),
                pl.BlockSpec((None, D, tf), lambda w, j, ie, ib, nb, *_: (ie[w], 0, nj + jt(j, nb[w]))),
                pl.BlockSpec((None, tf, D), lambda w, j, ie, ib, nb, *_: (ie[w], jt(j, nb[w]), 0)),
                pl.BlockSpec((None, 1, tf), lambda w, j, ie, ib, nb, *_: (ie[w], 0, jt(j, nb[w]))),
                pl.BlockSpec((None, 1, tf), lambda w, j, ie, ib, nb, *_: (ie[w], 0, nj + jt(j, nb[w]))),
                pl.BlockSpec((None, 1, D), lambda w, j, ie, *_: (ie[w], 0, 0)),
            ],
            out_specs=pl.BlockSpec(memory_space=pl.ANY),
            scratch_shapes=[
                pltpu.VMEM((2, rows, D // 2), jnp.uint32), pltpu.VMEM((rows, D), F32),
                pltpu.VMEM((D, tf), BF16), pltpu.VMEM((D, tf), BF16), pltpu.VMEM((tf, D), BF16),
                pltpu.SemaphoreType.DMA((2,)), pltpu.SemaphoreType.DMA,
            ],
        ),
        compiler_params=pltpu.CompilerParams(
            dimension_semantics=("arbitrary", "arbitrary"), vmem_limit_bytes=60 * MIB),
        name="experts",
    )(item_expert, item_blk0, item_nblk, n_blocks_used, x_sorted,
      w_gate_up, w_gate_up, w_down, b_gu, b_gu, b_d)


def _combine_kernel(dest_ref, h_ref, tg_ref, y_ref, o_ref, buf, sem):
    i = pl.program_id(0)
    n = pl.num_programs(0)
    tm = h_ref.shape[0]

    def issue_tile(step):
        slot = lax.rem(step, 2)

        def issue(r, _):
            for t in range(ROWS_PER_ISSUE):
                for k in range(TOP_K):
                    row = dest_ref[(step * tm + ROWS_PER_ISSUE * r + t) * TOP_K + k]
                    pltpu.make_async_copy(y_ref.at[pl.ds(row, 1), :],
                                          buf.at[slot, k, pl.ds(ROWS_PER_ISSUE * r + t, 1), :],
                                          sem.at[slot]).start()
            return 0

        lax.fori_loop(0, tm // ROWS_PER_ISSUE, issue, 0)

    @pl.when(i == 0)
    def _():
        issue_tile(0)

    @pl.when(i + 1 < n)
    def _():
        issue_tile(i + 1)

    slot = lax.rem(i, 2)
    for k in range(TOP_K):
        pltpu.make_async_copy(y_ref.at[pl.ds(0, tm), :], buf.at[slot, k], sem.at[slot]).wait()
    tg = tg_ref[...]
    acc = h_ref[...]
    for k in range(TOP_K):
        acc = acc + tg[:, k:k + 1] * buf[slot, k]
    o_ref[...] = acc


def _combine(h, top_g, y_sorted, dest, tm):
    T, D = h.shape
    return pl.pallas_call(
        _combine_kernel,
        out_shape=jax.ShapeDtypeStruct((T, D), F32),
        grid_spec=pltpu.PrefetchScalarGridSpec(
            num_scalar_prefetch=1,
            grid=(T // tm,),
            in_specs=[
                pl.BlockSpec((tm, D), lambda i, d: (i, 0)),
                pl.BlockSpec((tm, LANES), lambda i, d: (i, 0)),
                pl.BlockSpec(memory_space=pl.ANY),
            ],
            out_specs=pl.BlockSpec((tm, D), lambda i, d: (i, 0)),
            scratch_shapes=[pltpu.VMEM((2, TOP_K, tm, D), F32), pltpu.SemaphoreType.DMA((2,))],
        ),
        compiler_params=pltpu.CompilerParams(dimension_semantics=("arbitrary",)),
        name="combine",
    )(dest, h, top_g, y_sorted)


def _dispatch_kernel(dest_ref, row_end_ref, blk_end_ref, u_ref, x_ref, buf, zbuf, sem_l, sem_s, sem_z):
    i = pl.program_id(0)
    n = pl.num_programs(0)
    tm = buf.shape[1]
    R = MOE_ROWS
    n_blocks = x_ref.shape[0] // R

    def load(step):
        s = lax.rem(step, 2)
        return pltpu.make_async_copy(u_ref.at[pl.ds(step * tm, tm), :], buf.at[s], sem_l.at[s])

    def zero_copy(blk):
        return pltpu.make_async_copy(zbuf, x_ref.at[pl.ds(blk * R, R), :], sem_z)

    def wait_scatter(step):
        s = lax.rem(step, 2)
        for _ in range(TOP_K):
            pltpu.make_async_copy(u_ref.at[pl.ds(0, tm), :], buf.at[s], sem_s.at[s]).wait()

    @pl.when(i == 0)
    def _():
        zbuf[...] = jnp.zeros(zbuf.shape, zbuf.dtype)

        def partial_blocks(fn):
            def body(e, _):
                @pl.when(lax.rem(row_end_ref[e], R) != 0)
                def _():
                    fn(blk_end_ref[e] - 1)
                return 0
            lax.fori_loop(0, N_EXPERTS, body, 0)

        def unused_blocks(fn):
            def body(b, _):
                fn(b)
                return 0
            lax.fori_loop(blk_end_ref[N_EXPERTS - 1], n_blocks, body, 0)

        partial_blocks(lambda b: zero_copy(b).start())
        unused_blocks(lambda b: zero_copy(b).start())
        partial_blocks(lambda b: zero_copy(b).wait())
        unused_blocks(lambda b: zero_copy(b).wait())
        load(0).start()

    @pl.when(i > 0)
    def _():
        wait_scatter(i - 1)

    @pl.when(i + 1 < n)
    def _():
        load(i + 1).start()

    load(i).wait()
    slot = lax.rem(i, 2)

    def issue(r, _):
        for t in range(ROWS_PER_ISSUE):
            for k in range(TOP_K):
                row = dest_ref[(i * tm + ROWS_PER_ISSUE * r + t) * TOP_K + k]
                pltpu.make_async_copy(buf.at[slot, pl.ds(ROWS_PER_ISSUE * r + t, 1), :],
                                      x_ref.at[pl.ds(row, 1), :], sem_s.at[slot]).start()
        return 0

    lax.fori_loop(0, tm // ROWS_PER_ISSUE, issue, 0)

    @pl.when(i == n - 1)
    def _():
        wait_scatter(i)


def _dispatch(u_packed, dest, row_end, blk_end, P, tm):
    T, W = u_packed.shape
    return pl.pallas_call(
        _dispatch_kernel,
        out_shape=jax.ShapeDtypeStruct((P, W), u_packed.dtype),
        grid_spec=pltpu.PrefetchScalarGridSpec(
            num_scalar_prefetch=3,
            grid=(T // tm,),
            in_specs=[pl.BlockSpec(memory_space=pl.ANY)],
            out_specs=pl.BlockSpec(memory_space=pl.ANY),
            scratch_shapes=[pltpu.VMEM((2, tm, W), u_packed.dtype), pltpu.VMEM((MOE_ROWS, W), u_packed.dtype),
                            pltpu.SemaphoreType.DMA((2,)), pltpu.SemaphoreType.DMA((2,)),
                            pltpu.SemaphoreType.DMA],
        ),
        compiler_params=pltpu.CompilerParams(dimension_semantics=("arbitrary",)),
        name="dispatch",
    )(dest, row_end, blk_end, u_packed)


def _routing(top_i, expert_counts, T):
    R = MOE_ROWS
    n_assign = T * TOP_K
    e_flat = top_i[:, :TOP_K].reshape(n_assign)
    rank = top_i[:, TOP_K:2 * TOP_K].reshape(n_assign)
    counts = expert_counts[0, :N_EXPERTS].astype(jnp.int32)
    nblk = (counts + R - 1) // R
    blk_end = jnp.cumsum(nblk)
    blk_start = blk_end - nblk
    dest = blk_start[e_flat] * R + rank
    max_blocks = n_assign // R + N_EXPERTS
    P = max_blocks * R
    dest = dest.astype(jnp.int32)
    row_end = (blk_start * R + counts).astype(jnp.int32)
    nb_item = MOE_ITEM_BLOCKS
    n_items_e = (nblk + nb_item - 1) // nb_item
    item_end = jnp.cumsum(n_items_e)
    item_start = item_end - n_items_e
    max_items = (max_blocks + (nb_item - 1) * N_EXPERTS) // nb_item
    w = jnp.arange(max_items, dtype=jnp.int32)
    ie = jnp.minimum(jnp.searchsorted(item_end, w, side='right'), N_EXPERTS - 1).astype(jnp.int32)
    local = w - item_start[ie]
    live = w < item_end[-1]
    ib = blk_start[ie] + local * nb_item
    inb = jnp.clip(nblk[ie] - local * nb_item, 0, nb_item)
    last_e = ie[jnp.maximum(item_end[-1] - 1, 0)]
    ie = jnp.where(live, ie, last_e).astype(jnp.int32)
    ib = jnp.where(live, ib, 0).astype(jnp.int32)
    inb = jnp.where(live, inb, 0).astype(jnp.int32)
    return dest, row_end, blk_end.astype(jnp.int32), P, ie, ib, inb


def _layer(x, attn_norm_w, w_in, q_norm_w, k_norm_w, attn_sinks, conv_w, a_log, dt_bias,
           dn_norm_w, w_out, ffn_norm_w, w_router, b_router, w_gate_up, b_gate_up, w_down, b_down):
    B, S, D = x.shape
    T = B * S
    x2d = x.reshape(T, D)
    proj = _inproj(x2d, attn_norm_w, w_in, tm=min(T, 1024), tn=1024)
    attn_o = _attention(proj, attn_sinks, q_norm_w, k_norm_w, B, S)
    dn_o = _deltanet(proj, conv_w, a_log, dt_bias, dn_norm_w, B, S)
    h, u, top_i, top_g, expert_counts = _outproj_router(attn_o, dn_o, x2d, w_out, ffn_norm_w, w_router,
                                                        b_router, tm=min(T, 512))
    dest, row_end, blk_end, P, ie, ib, inb = _routing(top_i, expert_counts, T)
    x_sorted = _dispatch(u, dest, row_end, blk_end, P, tm=min(T, 512))
    y_sorted = _experts(x_sorted, ie, ib, inb, blk_end[-1:], w_gate_up, b_gate_up, w_down, b_down)
    out = _combine(h, top_g, y_sorted, dest, tm=min(T, 256))
    return out.reshape(B, S, D)


def kernel(x, attn_norm_w, w_in, q_norm_w, k_norm_w, attn_sinks, conv_w, a_log, dt_bias, dn_norm_w,
           w_out, ffn_norm_w, w_router, b_router, w_gate_up, b_gate_up, w_down, b_down):
    h = x
    for l in range(attn_norm_w.shape[0]):
        h = _layer(h, attn_norm_w[l], w_in[l], q_norm_w[l], k_norm_w[l], attn_sinks[l], conv_w[l],
                   a_log[l], dt_bias[l], dn_norm_w[l], w_out[l], ffn_norm_w[l], w_router[l],
                   b_router[l], w_gate_up[l], b_gate_up[l], w_down[l], b_down[l])
    return h
```
